```python
import math
import jax, jax.numpy as jnp
from jax import lax
import numpy as np

D_MODEL = 1024
BATCH = 2
SEQ = 16384
DEPTH = 4
DEC_BATCH = 8
DEC_SEQ = 32
PAST_LEN = 1024

CHUNK = 64
Q_BLOCK = 128
POOL_WINDOWS = (2, 4, 8, 16)
N_POOL_GROUPS = 4
POOL_DIM = D_MODEL // 4
POOL_GROUP_DIM = POOL_DIM // N_POOL_GROUPS
POOL_CTX = max(POOL_WINDOWS) - 1
QK_NOPE_DIM = 64
QK_ROPE_DIM = 32
V_HEAD_DIM = 64
N_HEADS = (D_MODEL - POOL_DIM) // V_HEAD_DIM
Q_LORA_RANK = 384
KV_LORA_RANK = 256
ROPE_THETA = 10000.0
MLA_DIM = N_HEADS * V_HEAD_DIM
D_MIX = POOL_DIM + MLA_DIM
D_IN = POOL_DIM + Q_LORA_RANK + KV_LORA_RANK + QK_ROPE_DIM
SOFTMAX_SCALE = (QK_NOPE_DIM + QK_ROPE_DIM) ** -0.5
N_EXPERTS = 16
N_EXPERT_GROUPS = 4
EXPERTS_PER_GROUP = N_EXPERTS // N_EXPERT_GROUPS
TOP_K = 2
D_EXPERT = 512
DEEPNORM_ALPHA = (2 * DEPTH) ** 0.25
DEEPNORM_BETA = (8 * DEPTH) ** -0.25
LN_EPS = 1e-5
RMS_EPS = 1e-6

kernel_name = "hybrid_pool_mla_moe_streaming_encoder_step"


def layer_norm(x, g, b):
    xf = x.astype(jnp.float32)
    mu = jnp.mean(xf, -1, keepdims=True)
    var = jnp.mean(jnp.square(xf - mu), -1, keepdims=True)
    return ((xf - mu) * lax.rsqrt(var + LN_EPS) * g.astype(jnp.float32) + b.astype(jnp.float32)).astype(x.dtype)


def rms_norm(x, g):
    xf = x.astype(jnp.float32)
    return (xf * lax.rsqrt(jnp.mean(xf * xf, -1, keepdims=True) + RMS_EPS) * g.astype(jnp.float32)).astype(x.dtype)


def rope(x, pos):
    half = QK_ROPE_DIM // 2
    inv_freq = ROPE_THETA ** (-jnp.arange(half, dtype=jnp.float32) / half)
    ang = pos.astype(jnp.float32)[:, None] * inv_freq[None, :]
    ang = ang.reshape(ang.shape[:1] + (1,) * (x.ndim - 3) + ang.shape[1:])
    cos, sin = jnp.cos(ang), jnp.sin(ang)
    xf = x.astype(jnp.float32)
    x1, x2 = xf[..., :half], xf[..., half:]
    return jnp.concatenate([x1 * cos - x2 * sin, x1 * sin + x2 * cos], -1).astype(x.dtype)


def multiscale_pool(u, left, pos, w_pool, pool_scale):
    B, T, _ = u.shape
    ext = jnp.concatenate([left, u], 1).astype(jnp.float32)
    cs = jnp.concatenate([jnp.zeros((B, 1, POOL_DIM), jnp.float32), jnp.cumsum(ext, 1)], 1)
    end = cs[:, POOL_CTX + 1:]
    means = []
    for g, w in enumerate(POOL_WINDOWS):
        sl = slice(g * POOL_GROUP_DIM, (g + 1) * POOL_GROUP_DIM)
        start = cs[:, POOL_CTX + 1 - w:POOL_CTX + 1 - w + T, sl]
        cnt = jnp.minimum(w, pos + 1).astype(jnp.float32)[None, :, None]
        means.append((end[..., sl] - start) / cnt)
    pooled = (jnp.concatenate(means, -1) - u.astype(jnp.float32)).astype(u.dtype)
    pooled = pooled.reshape(B, T, N_POOL_GROUPS, POOL_GROUP_DIM)
    y = jnp.einsum('btgc,gcd->btgd', pooled, w_pool).reshape(B, T, POOL_DIM)
    return y * pool_scale


def latent_attention(q_nope, q_pe, ckv, kpe, w_uk, w_uv, mask):
    q_lat = jnp.einsum('bqhn,khn->bqhk', q_nope, w_uk)
    s = (jnp.einsum('bqhk,bsk->bhqs', q_lat, ckv)
         + jnp.einsum('bqhr,bsr->bhqs', q_pe, kpe)).astype(jnp.float32) * SOFTMAX_SCALE
    if mask is not None:
        s = jnp.where(mask, s, -jnp.inf)
    p = jax.nn.softmax(s, -1).astype(ckv.dtype)
    o_lat = jnp.einsum('bhqs,bsk->bqhk', p, ckv)
    return jnp.einsum('bqhk,khv->bqhv', o_lat, w_uv)


def prompt_attention(q_nope, q_pe, ckv, kpe, w_uk, w_uv):
    B, T = q_nope.shape[:2]
    nb = T // Q_BLOCK
    key_chunk = jnp.arange(T) // CHUNK
    qn = q_nope.reshape(B, nb, Q_BLOCK, N_HEADS, QK_NOPE_DIM).swapaxes(0, 1)
    qp = q_pe.reshape(B, nb, Q_BLOCK, N_HEADS, QK_ROPE_DIM).swapaxes(0, 1)

    def block(args):
        qn_b, qp_b, i = args
        q_chunk = (i * Q_BLOCK + jnp.arange(Q_BLOCK)) // CHUNK
        mask = key_chunk[None, :] <= q_chunk[:, None]
        return latent_attention(qn_b, qp_b, ckv, kpe, w_uk, w_uv, mask)

    o = lax.map(block, (qn, qp, jnp.arange(nb)))
    return o.swapaxes(0, 1).reshape(B, T, N_HEADS, V_HEAD_DIM)


def token_mixer(h, pos, pool_left, kv_past, pe_past, w_in, q_norm_g, kv_norm_g, w_uq, w_uk, w_uv, w_pool, pool_scale, w_o):
    B, T, _ = h.shape
    z = jnp.einsum('btd,df->btf', h, w_in)
    u, cq, ckv_raw, kpe_raw = jnp.split(
        z, [POOL_DIM, POOL_DIM + Q_LORA_RANK, POOL_DIM + Q_LORA_RANK + KV_LORA_RANK], -1)
    y_pool = multiscale_pool(u, pool_left, pos, w_pool, pool_scale)
    q = jnp.einsum('btr,rf->btf', rms_norm(cq, q_norm_g), w_uq).reshape(B, T, N_HEADS, QK_NOPE_DIM + QK_ROPE_DIM)
    q_nope, q_pe = q[..., :QK_NOPE_DIM], rope(q[..., QK_NOPE_DIM:], pos)
    ckv = rms_norm(ckv_raw, kv_norm_g)
    kpe = rope(kpe_raw, pos)
    if kv_past is None:
        y_att = prompt_attention(q_nope, q_pe, ckv, kpe, w_uk, w_uv)
    else:
        y_att = latent_attention(q_nope, q_pe, jnp.concatenate([kv_past, ckv], 1),
                                 jnp.concatenate([pe_past, kpe], 1), w_uk, w_uv, None)
    y = jnp.einsum('btf,fd->btd', jnp.concatenate([y_pool, y_att.reshape(B, T, MLA_DIM)], -1), w_o)
    new_pool = jnp.concatenate([pool_left, u], 1)[:, -POOL_CTX:]
    return y, new_pool, ckv, kpe


def route(h, w_router, router_bias):
    n = h.shape[0]
    scores = jax.nn.sigmoid((h @ w_router).astype(jnp.float32))
    sel = scores + router_bias.astype(jnp.float32)
    grp_score = lax.top_k(sel.reshape(n, N_EXPERT_GROUPS, EXPERTS_PER_GROUP), TOP_K)[0].sum(-1)
    best = jnp.argmax(grp_score, -1)
    in_grp = (jnp.arange(N_EXPERTS) // EXPERTS_PER_GROUP)[None, :] == best[:, None]
    _, idx = lax.top_k(jnp.where(in_grp, sel, -jnp.inf), TOP_K)
    w = jnp.take_along_axis(scores, idx, -1)
    w = w / jnp.sum(w, -1, keepdims=True)
    return jnp.sum(jax.nn.one_hot(idx, N_EXPERTS, dtype=jnp.float32) * w[..., None], 1)


def moe(h, gates, w_gate, w_up, w_down):
    y = jnp.zeros_like(h)
    for e in range(N_EXPERTS):
        a = jax.nn.silu(h @ w_gate[e]) * (h @ w_up[e])
        y = y + gates[:, e:e + 1].astype(h.dtype) * (a @ w_down[e])
    return y


def encoder(x, c, pos, pool_ctx, kv_past, pe_past, weights):
    (ln_in_g, ln_in_b, w_ada, b_ada, w_in, q_norm_g, kv_norm_g, w_uq, w_uk, w_uv, w_pool, pool_scale, w_o,
     ln1_g, ln1_b, w_router, router_bias, w_gate, w_up, w_down, ln2_g, ln2_b) = weights
    B, T, D = x.shape
    x = layer_norm(x, ln_in_g, ln_in_b)
    new_kv, new_pe, new_pool = [], [], []
    for l in range(DEPTH):
        mod = jax.nn.silu(c) @ w_ada[l] + b_ada[l]
        sh1, sc1, g1, sh2, sc2, g2 = jnp.split(mod[:, None, :], 6, -1)
        h = x * (1 + sc1) + sh1
        y, pool_l, ckv, kpe = token_mixer(
            h, pos, pool_ctx[l], None if kv_past is None else kv_past[l], None if pe_past is None else pe_past[l],
            w_in[l], q_norm_g[l], kv_norm_g[l], w_uq[l], w_uk[l], w_uv[l], w_pool[l], pool_scale[l], w_o[l])
        x = layer_norm(DEEPNORM_ALPHA * x + (1 + g1) * y, ln1_g[l], ln1_b[l])
        h = (x * (1 + sc2) + sh2).reshape(B * T, D)
        gates = route(h, w_router, router_bias)
        y = moe(h, gates, w_gate[l], w_up[l], w_down[l]).reshape(B, T, D)
        x = layer_norm(DEEPNORM_ALPHA * x + (1 + g2) * y, ln2_g[l], ln2_b[l])
        new_kv.append(ckv)
        new_pe.append(kpe)
        new_pool.append(pool_l)
    return x, jnp.stack(new_kv), jnp.stack(new_pe), jnp.stack(new_pool)


def setup_inputs(seed: int = 0) -> dict:
    key = jax.random.key(seed)
    ks = jax.random.split(key, 32)
    D = D_MODEL

    def nrm(k, shape, s):
        return jax.random.normal(k, shape, jnp.float32) * s

    return {
        "x_prompt": nrm(ks[0], (BATCH, SEQ, D), 1.0),
        "x_sample": nrm(ks[1], (DEC_BATCH, DEC_SEQ, D), 1.0),
        "cache_kv": nrm(ks[2], (DEPTH, DEC_BATCH, PAST_LEN, KV_LORA_RANK), 1.0),
        "cache_pe": nrm(ks[3], (DEPTH, DEC_BATCH, PAST_LEN, QK_ROPE_DIM), 1.0),
        "state_pool": nrm(ks[4], (DEPTH, DEC_BATCH, POOL_CTX, POOL_DIM), 1.0),
        "c_prompt": nrm(ks[5], (BATCH, D), 1.0),
        "c_sample": nrm(ks[6], (DEC_BATCH, D), 1.0),
        "ln_in_g": 1.0 + nrm(ks[7], (D,), 0.02),
        "ln_in_b": nrm(ks[8], (D,), 0.02),
        "w_ada": nrm(ks[9], (DEPTH, D, 6 * D), 0.5 * D ** -0.5),
        "b_ada": nrm(ks[10], (DEPTH, 6 * D), 0.02),
        "w_in": nrm(ks[11], (DEPTH, D, D_IN), D ** -0.5),
        "q_norm_g": 1.0 + nrm(ks[12], (DEPTH, Q_LORA_RANK), 0.02),
        "kv_norm_g": 1.0 + nrm(ks[13], (DEPTH, KV_LORA_RANK), 0.02),
        "w_uq": nrm(ks[14], (DEPTH, Q_LORA_RANK, N_HEADS * (QK_NOPE_DIM + QK_ROPE_DIM)), Q_LORA_RANK ** -0.5),
        "w_uk": nrm(ks[15], (DEPTH, KV_LORA_RANK, N_HEADS, QK_NOPE_DIM), KV_LORA_RANK ** -0.5),
        "w_uv": nrm(ks[16], (DEPTH, KV_LORA_RANK, N_HEADS, V_HEAD_DIM), KV_LORA_RANK ** -0.5),
        "w_pool": nrm(ks[17], (DEPTH, N_POOL_GROUPS, POOL_GROUP_DIM, POOL_GROUP_DIM), POOL_GROUP_DIM ** -0.5),
        "pool_scale": 1.0 + nrm(ks[18], (DEPTH, POOL_DIM), 0.1),
        "w_o": nrm(ks[19], (DEPTH, D_MIX, D), DEEPNORM_BETA * D_MIX ** -0.5),
        "ln1_g": 1.0 + nrm(ks[20], (DEPTH, D), 0.02),
        "ln1_b": nrm(ks[21], (DEPTH, D), 0.02),
        "w_router": nrm(ks[22], (D, N_EXPERTS), D ** -0.5),
        "router_bias": nrm(ks[23], (N_EXPERTS,), 0.01),
        "w_gate": nrm(ks[24], (DEPTH, N_EXPERTS, D, D_EXPERT), D ** -0.5),
        "w_up": nrm(ks[25], (DEPTH, N_EXPERTS, D, D_EXPERT), D ** -0.5),
        "w_down": nrm(ks[26], (DEPTH, N_EXPERTS, D_EXPERT, D), DEEPNORM_BETA * D_EXPERT ** -0.5),
        "ln2_g": 1.0 + nrm(ks[27], (DEPTH, D), 0.02),
        "ln2_b": nrm(ks[28], (DEPTH, D), 0.02),
    }


def reference(x_prompt, x_sample, cache_kv, cache_pe, state_pool, c_prompt, c_sample,
              ln_in_g, ln_in_b, w_ada, b_ada, w_in, q_norm_g, kv_norm_g, w_uq, w_uk, w_uv, w_pool, pool_scale, w_o,
              ln1_g, ln1_b, w_router, router_bias, w_gate, w_up, w_down, ln2_g, ln2_b):
    weights = (ln_in_g, ln_in_b, w_ada, b_ada, w_in, q_norm_g, kv_norm_g, w_uq, w_uk, w_uv, w_pool, pool_scale, w_o,
               ln1_g, ln1_b, w_router, router_bias, w_gate, w_up, w_down, ln2_g, ln2_b)
    n_prompt, t_prompt = x_prompt.shape[0], x_prompt.shape[1]
    past_len = cache_kv.shape[2]
    pos_prompt = jnp.arange(t_prompt, dtype=jnp.int32)
    pos_sample = past_len + jnp.arange(x_sample.shape[1], dtype=jnp.int32)
    pool_zero = jnp.zeros((DEPTH, n_prompt, POOL_CTX, POOL_DIM), x_prompt.dtype)
    y_prompt, kv_prompt, pe_prompt, pool_prompt = encoder(
        x_prompt, c_prompt, pos_prompt, pool_zero, None, None, weights)
    y_sample, kv_sample, pe_sample, pool_sample = encoder(
        x_sample, c_sample, pos_sample, state_pool, cache_kv, cache_pe, weights)
    return (y_prompt, y_sample, kv_prompt, pe_prompt, pool_prompt, kv_sample, pe_sample, pool_sample)
```

```python
import functools

import jax
import jax.numpy as jnp
from jax import lax
from jax.experimental import pallas as pl
from jax.experimental.pallas import tpu as pltpu

F32 = jnp.float32
BF16 = jnp.bfloat16

CHUNK = 64
POOL_WINDOWS = (2, 4, 8, 16)
POOL_DIM = 256
POOL_GROUP_DIM = 64
POOL_CTX = 15
HALO = 16
QK_NOPE_DIM = 64
QK_ROPE_DIM = 32
V_HEAD_DIM = 64
N_HEADS = 12
Q_LORA_RANK = 384
KV_LORA_RANK = 256
ROPE_THETA = 10000.0
SOFTMAX_SCALE = (QK_NOPE_DIM + QK_ROPE_DIM) ** -0.5
N_EXPERTS = 16
N_EXPERT_GROUPS = 4
EXPERTS_PER_GROUP = 4
D_EXPERT = 512
LN_EPS = 1e-5
RMS_EPS = 1e-6

HEAD_LANES = 128
ROPE_OFF = QK_NOPE_DIM
QK_LANES = N_HEADS * HEAD_LANES
V_LANES = N_HEADS * V_HEAD_DIM
Z_LANES = POOL_DIM + Q_LORA_RANK + KV_LORA_RANK + 2 * HEAD_LANES
GATE_LANES = 128

VMEM_LIMIT = 56 * 1024 * 1024


def _cparams(sem):
    return pltpu.CompilerParams(dimension_semantics=sem, vmem_limit_bytes=VMEM_LIMIT)


def _layer_norm(x, g, b):
    mu = jnp.mean(x, -1, keepdims=True)
    xc = x - mu
    var = jnp.mean(xc * xc, -1, keepdims=True)
    return xc * lax.rsqrt(var + LN_EPS) * g + b


def _rms_norm(x, g):
    return x * lax.rsqrt(jnp.mean(x * x, -1, keepdims=True) + RMS_EPS) * g


def _dot(a, b):
    return jnp.dot(a, b, preferred_element_type=F32)


def _dot_nt(a, b):
    return lax.dot_general(a, b, (((1,), (1,)), ((), ())), preferred_element_type=F32)


def _split_bf16(x):
    hi = x.astype(BF16)
    lo = (x - hi.astype(F32)).astype(BF16)
    return hi, lo


def _ada_kernel(c_ref, w_ref, b_ref, o_ref):
    c = c_ref[...]
    s = c * jax.nn.sigmoid(c)
    s_hi, s_lo = _split_bf16(s)
    w_hi, w_lo = _split_bf16(w_ref[0])
    o_ref[0] = _dot(s_hi, w_hi) + _dot(s_lo, w_hi) + _dot(s_hi, w_lo) + b_ref[0]


def _ada_call(c_all, w_ada, b_ada):
    depth, d, n = w_ada.shape
    rows = c_all.shape[0]
    tn = 1536
    return pl.pallas_call(
        _ada_kernel,
        grid=(depth, n // tn),
        in_specs=[pl.BlockSpec((rows, d), lambda l, j: (0, 0)),
                  pl.BlockSpec((1, d, tn), lambda l, j: (l, 0, j)),
                  pl.BlockSpec((1, 1, tn), lambda l, j: (l, 0, j))],
        out_specs=pl.BlockSpec((1, rows, tn), lambda l, j: (l, 0, j)),
        out_shape=jax.ShapeDtypeStruct((depth, rows, n), F32),
        name="ada_mod",
        compiler_params=_cparams(("arbitrary", "arbitrary")),
    )(c_all, w_ada, b_ada.reshape(depth, 1, n))


def _pre_kernel(*refs, first):
    if first:
        (x_ref, mod_ref, lng_ref, lnb_ref, w_in_ref, qg_ref, kvg_ref, w_uq_ref, w_uqr_ref, w_uk_ref, w_uv_ref,
         cos_ref, sin_ref, xn_ref, u_ref, ckv_ref, kpe_ref, q_ref, k_ref, v_ref) = refs
    else:
        (x_ref, mod_ref, w_in_ref, qg_ref, kvg_ref, w_uq_ref, w_uqr_ref, w_uk_ref, w_uv_ref,
         cos_ref, sin_ref, u_ref, ckv_ref, kpe_ref, q_ref, k_ref, v_ref) = refs
    x = x_ref[0]
    if first:
        x = _layer_norm(x, lng_ref[...], lnb_ref[...])
        xn_ref[0] = x
    sh1 = mod_ref[0, 0:1, :]
    sc1 = mod_ref[0, 1:2, :]
    h = (x * (1.0 + sc1) + sh1).astype(BF16)
    z = _dot(h, w_in_ref[...])
    o_cq = POOL_DIM
    o_kv = o_cq + Q_LORA_RANK
    o_pe = o_kv + KV_LORA_RANK
    u_ref[0] = z[:, :POOL_DIM]
    cqn = _rms_norm(z[:, o_cq:o_kv], qg_ref[...])
    ckv = _rms_norm(z[:, o_kv:o_pe], kvg_ref[...])
    ckv_ref[0] = ckv
    cos = cos_ref[...]
    sin = sin_ref[...]
    kpe = z[:, o_pe:o_pe + HEAD_LANES] * cos + z[:, o_pe + HEAD_LANES:] * sin
    kpe_ref[0] = kpe
    cqb = cqn.astype(BF16)
    ckvb = ckv.astype(BF16)
    qa = _dot(cqb, w_uq_ref[...])
    qb = _dot(cqb, w_uqr_ref[...])
    kn = _dot(ckvb, w_uk_ref[...])
    v_ref[0] = _dot(ckvb, w_uv_ref[...]).astype(BF16)
    for hd in range(N_HEADS):
        sl = slice(hd * HEAD_LANES, (hd + 1) * HEAD_LANES)
        q_ref[0, :, sl] = ((qa[:, sl] * cos + qb[:, sl] * sin) * SOFTMAX_SCALE).astype(BF16)
        k_ref[0, :, sl] = (kn[:, sl] + kpe).astype(BF16)


def _pre_call(x, mod, ln_in, w, cos, sin, *, first, tt):
    bsz, t, d = x.shape
    grid = (bsz, t // tt)
    tok = lambda n: pl.BlockSpec((1, tt, n), lambda b, i: (b, i, 0))
    full = lambda a: pl.BlockSpec(a.shape, lambda b, i: (0,) * a.ndim)
    in_arrays = [x, mod]
    in_specs = [tok(d), pl.BlockSpec((1, 8, d), lambda b, i: (b, 0, 0))]
    if first:
        in_arrays += list(ln_in)
        in_specs += [full(a) for a in ln_in]
    in_arrays += list(w)
    in_specs += [full(a) for a in w]
    in_arrays += [cos, sin]
    in_specs += [pl.BlockSpec((tt, HEAD_LANES), lambda b, i: (i, 0))] * 2
    out_shape, out_specs = [], []
    if first:
        out_shape.append(jax.ShapeDtypeStruct((bsz, t, d), F32))
        out_specs.append(tok(d))
    for n, dt in ((POOL_DIM, F32), (KV_LORA_RANK, F32), (HEAD_LANES, F32), (QK_LANES, BF16), (QK_LANES, BF16),
                  (V_LANES, BF16)):
        out_shape.append(jax.ShapeDtypeStruct((bsz, t, n), dt))
        out_specs.append(tok(n))
    return pl.pallas_call(
        functools.partial(_pre_kernel, first=first),
        grid=grid, in_specs=in_specs, out_specs=out_specs, out_shape=out_shape,
        name="pre_mixer",
        compiler_params=_cparams(("arbitrary", "arbitrary")),
    )(*in_arrays)


def _softmax_step(q, k, v, carry, mask):
    m, l, acc = carry
    s = _dot_nt(q, k)
    if mask is not None:
        s = jnp.where(mask, s, -jnp.inf)
    m_new = jnp.maximum(m, jnp.max(s, -1, keepdims=True))
    p = jnp.exp(s - m_new)
    alpha = jnp.exp(m - m_new)
    l = alpha * l + jnp.sum(p, -1, keepdims=True)
    acc = alpha * acc + _dot(p.astype(BF16), v)
    return m_new, l, acc


def _prompt_attn_kernel(q_ref, k_ref, v_ref, o_ref, *, tq):
    qi = pl.program_id(2)
    row_chunk = lax.broadcasted_iota(jnp.int32, (tq, tq), 0) // CHUNK
    col_chunk = lax.broadcasted_iota(jnp.int32, (tq, tq), 1) // CHUNK
    diag_mask = col_chunk <= row_chunk
    outs = []
    for hh in range(2):
        hs = slice(hh * HEAD_LANES, (hh + 1) * HEAD_LANES)
        q = q_ref[0, :, hs]

        def kv_at(j):
            start = pl.multiple_of(j * tq, tq)
            return k_ref[0, pl.ds(start, tq), hs], v_ref[0, pl.ds(start, tq), :]

        def body(j, carry):
            k, v = kv_at(j)
            return _softmax_step(q, k, v, carry, None)

        init = (jnp.full((tq, 1), -jnp.inf, F32), jnp.zeros((tq, 1), F32), jnp.zeros((tq, 2 * V_HEAD_DIM), F32))
        carry = lax.fori_loop(0, qi, body, init)
        k, v = kv_at(qi)
        m, l, acc = _softmax_step(q, k, v, carry, diag_mask)
        outs.append(acc / l)
    lane = lax.broadcasted_iota(jnp.int32, (tq, 2 * V_HEAD_DIM), 1)
    o_ref[0] = jnp.where(lane < V_HEAD_DIM, outs[0], outs[1]).astype(o_ref.dtype)


def _prompt_attn_call(q, k, v, *, tq):
    bsz, t, _ = q.shape
    pair_q = 2 * HEAD_LANES
    pair_v = 2 * V_HEAD_DIM
    return pl.pallas_call(
        functools.partial(_prompt_attn_kernel, tq=tq),
        grid=(bsz, N_HEADS // 2, t // tq),
        in_specs=[pl.BlockSpec((1, tq, pair_q), lambda b, hp, i: (b, i, hp)),
                  pl.BlockSpec((1, t, pair_q), lambda b, hp, i: (b, 0, hp)),
                  pl.BlockSpec((1, t, pair_v), lambda b, hp, i: (b, 0, hp))],
        out_specs=pl.BlockSpec((1, tq, pair_v), lambda b, hp, i: (b, i, hp)),
        out_shape=jax.ShapeDtypeStruct((bsz, t, V_LANES), BF16),
        name="attn_prompt",
        compiler_params=_cparams(("arbitrary", "arbitrary", "arbitrary")),
    )(q, k, v)


def _sample_attn_kernel(q_ref, kn_ref, vn_ref, ckv_ref, cpe_ref, w_uk_ref, w_uv_ref, o_ref):
    ckvb = ckv_ref[0, 0].astype(BF16)
    cpe = cpe_ref[0, 0]
    vc = _dot(ckvb, w_uv_ref[...]).astype(BF16)
    tq = q_ref.shape[1]
    lane = lax.broadcasted_iota(jnp.int32, (tq, 2 * V_HEAD_DIM), 1)
    for hp in range(N_HEADS // 2):
        vs = slice(hp * 2 * V_HEAD_DIM, (hp + 1) * 2 * V_HEAD_DIM)
        v_c = vc[:, vs]
        v_n = vn_ref[0, :, vs]
        outs = []
        for hh in range(2):
            hd = 2 * hp + hh
            hs = slice(hd * HEAD_LANES, (hd + 1) * HEAD_LANES)
            q = q_ref[0, :, hs]
            k_c = (_dot(ckvb, w_uk_ref[:, hs]) + cpe).astype(BF16)
            k_n = kn_ref[0, :, hs]
            s_c = _dot_nt(q, k_c)
            s_n = _dot_nt(q, k_n)
            m = jnp.maximum(jnp.max(s_c, -1, keepdims=True), jnp.max(s_n, -1, keepdims=True))
            p_c = jnp.exp(s_c - m)
            p_n = jnp.exp(s_n - m)
            l = jnp.sum(p_c, -1, keepdims=True) + jnp.sum(p_n, -1, keepdims=True)
            outs.append((_dot(p_c.astype(BF16), v_c) + _dot(p_n.astype(BF16), v_n)) / l)
        o_ref[0, :, vs] = jnp.where(lane < V_HEAD_DIM, outs[0], outs[1]).astype(o_ref.dtype)


def _sample_attn_call(q, kn, vn, cache_kv, cache_pe, w_uk, w_uv, layer):
    bsz, t, _ = q.shape
    past = cache_kv.shape[2]
    tok = lambda n: pl.BlockSpec((1, t, n), lambda b: (b, 0, 0))
    full = lambda a: pl.BlockSpec(a.shape, lambda b: (0,) * a.ndim)
    return pl.pallas_call(
        _sample_attn_kernel,
        grid=(bsz,),
        in_specs=[tok(QK_LANES), tok(QK_LANES), tok(V_LANES),
                  pl.BlockSpec((1, 1, past, KV_LORA_RANK), lambda b: (layer, b, 0, 0)),
                  pl.BlockSpec((1, 1, past, HEAD_LANES), lambda b: (layer, b, 0, 0)),
                  full(w_uk), full(w_uv)],
        out_specs=tok(V_LANES),
        out_shape=jax.ShapeDtypeStruct((bsz, t, V_LANES), BF16),
        name="attn_sample",
        compiler_params=_cparams(("arbitrary",)),
    )(q, kn, vn, cache_kv, cache_pe, w_uk, w_uv)


def _route_gates(logits_t, bias_t):
    score = [jax.nn.sigmoid(logits_t[e:e + 1, :]) for e in range(N_EXPERTS)]
    sel = [score[e] + bias_t[e:e + 1, :] for e in range(N_EXPERTS)]
    grp = []
    for g in range(N_EXPERT_GROUPS):
        mem = sel[g * EXPERTS_PER_GROUP:(g + 1) * EXPERTS_PER_GROUP]
        best = None
        for a in range(EXPERTS_PER_GROUP):
            for b in range(a + 1, EXPERTS_PER_GROUP):
                pair = mem[a] + mem[b]
                best = pair if best is None else jnp.maximum(best, pair)
        grp.append(best)
    in_best = []
    for g in range(N_EXPERT_GROUPS):
        ok = None
        for g2 in range(N_EXPERT_GROUPS):
            if g2 == g:
                continue
            c = (grp[g] > grp[g2]) if g2 < g else (grp[g] >= grp[g2])
            ok = c if ok is None else (ok & c)
        in_best.append(ok)
    picked = []
    for e in range(N_EXPERTS):
        g = e // EXPERTS_PER_GROUP
        rank = jnp.zeros_like(sel[e], dtype=jnp.int32)
        for e2 in range(g * EXPERTS_PER_GROUP, (g + 1) * EXPERTS_PER_GROUP):
            if e2 == e:
                continue
            ahead = (sel[e2] >= sel[e]) if e2 < e else (sel[e2] > sel[e])
            rank = rank + ahead.astype(jnp.int32)
        picked.append(in_best[g] & (rank < 2))
    wsel = [jnp.where(picked[e], score[e], 0.0) for e in range(N_EXPERTS)]
    total = wsel[0]
    for e in range(1, N_EXPERTS):
        total = total + wsel[e]
    return [wsel[e] / total for e in range(N_EXPERTS)]


def _post_kernel(x_ref, u_ref, halo_ref, left_ref, att_ref, mod_ref, w_pool_ref, pscale_ref, w_o_ref,
                 ln_g_ref, ln_b_ref, w_r_ref, rb_ref, x1_ref, h2_ref, gate_ref, ext_ref, *, tt, pos0, alpha):
    i = pl.program_id(1)
    u = u_ref[0]
    halo = jnp.where(i == 0, left_ref[0], halo_ref[0])
    ext_ref[0:HALO, :] = halo
    ext_ref[HALO:HALO + tt, :] = u
    sums = {}
    run = u
    for j in range(1, max(POOL_WINDOWS)):
        run = run + ext_ref[HALO - j:HALO - j + tt, :]
        if j + 1 in POOL_WINDOWS:
            sums[j + 1] = run
    lane = lax.broadcasted_iota(jnp.int32, (tt, POOL_DIM), 1)
    grp = lane // POOL_GROUP_DIM
    win_sum = sums[POOL_WINDOWS[-1]]
    win_len = jnp.full((tt, POOL_DIM), POOL_WINDOWS[-1], jnp.int32)
    for g in range(len(POOL_WINDOWS) - 2, -1, -1):
        win_sum = jnp.where(grp == g, sums[POOL_WINDOWS[g]], win_sum)
        win_len = jnp.where(grp == g, POOL_WINDOWS[g], win_len)
    pos = pos0 + i * tt + lax.broadcasted_iota(jnp.int32, (tt, POOL_DIM), 0)
    cnt = jnp.minimum(win_len, pos + 1).astype(F32)
    pooled = (win_sum / cnt - u).astype(BF16)
    y_pool = _dot(pooled, w_pool_ref[...]) * pscale_ref[...]
    y = _dot(y_pool.astype(BF16), w_o_ref[0:POOL_DIM, :]) + _dot(att_ref[0], w_o_ref[POOL_DIM:, :])
    g1 = mod_ref[0, 2:3, :]
    sh2 = mod_ref[0, 3:4, :]
    sc2 = mod_ref[0, 4:5, :]
    x1 = _layer_norm(alpha * x_ref[0] + (1.0 + g1) * y, ln_g_ref[...], ln_b_ref[...])
    x1_ref[0] = x1
    h2 = x1 * (1.0 + sc2) + sh2
    h2_ref[0] = h2.astype(BF16)
    h_hi, h_lo = _split_bf16(h2)
    w_r = w_r_ref[...]
    l_hi = _dot_nt(w_r, h_hi)
    l_lo = _dot_nt(w_r[0:N_EXPERTS, :], h_lo)
    logits_t = l_hi[0:N_EXPERTS, :] + l_hi[N_EXPERTS:, :] + l_lo
    rows = _route_gates(logits_t, rb_ref[...])
    rows.append(jnp.zeros((GATE_LANES - N_EXPERTS, tt), F32))
    gate_ref[0] = jnp.concatenate(rows, axis=0).T


def _post_call(x, u, left, att, mod, w, *, tt, pos0, alpha):
    bsz, t, d = x.shape
    per_tile = tt // HALO
    tok = lambda n: pl.BlockSpec((1, tt, n), lambda b, i: (b, i, 0))
    full = lambda a: pl.BlockSpec(a.shape, lambda b, i: (0,) * a.ndim)
    in_specs = [tok(d), tok(POOL_DIM),
                pl.BlockSpec((1, HALO, POOL_DIM), lambda b, i: (b, jnp.maximum(i * per_tile - 1, 0), 0)),
                pl.BlockSpec((1, HALO, POOL_DIM), lambda b, i: (b, 0, 0)),
                tok(V_LANES), pl.BlockSpec((1, 8, d), lambda b, i: (b, 0, 0))] + [full(a) for a in w]
    out_shape = [jax.ShapeDtypeStruct((bsz, t, d), F32), jax.ShapeDtypeStruct((bsz, t, d), BF16),
                 jax.ShapeDtypeStruct((bsz, t, GATE_LANES), F32)]
    out_specs = [tok(d), tok(d), tok(GATE_LANES)]
    return pl.pallas_call(
        functools.partial(_post_kernel, tt=tt, pos0=pos0, alpha=alpha),
        grid=(bsz, t // tt), in_specs=in_specs, out_specs=out_specs, out_shape=out_shape,
        scratch_shapes=[pltpu.VMEM((HALO + tt, POOL_DIM), F32)],
        name="post_mixer",
        compiler_params=_cparams(("arbitrary", "arbitrary")),
    )(x, u, u, left, att, mod, *w)


def _moe_kernel(x1_ref, h2_ref, gate_ref, g2_ref, w_gu_ref, w_dn_ref, ln_g_ref, ln_b_ref, o_ref, acc_ref, *, alpha):
    e = pl.program_id(2)

    @pl.when(e == 0)
    def _():
        acc_ref[...] = jnp.zeros_like(acc_ref)

    gu = _dot(h2_ref[0], w_gu_ref[0])
    gate_part = gu[:, :D_EXPERT]
    a = gate_part * jax.nn.sigmoid(gate_part) * gu[:, D_EXPERT:]
    dn = _dot(a.astype(BF16), w_dn_ref[0])
    gates = gate_ref[0]
    lane = lax.broadcasted_iota(jnp.int32, gates.shape, 1)
    g_e = jnp.sum(jnp.where(lane == e, gates, 0.0), -1, keepdims=True)
    acc_ref[...] += g_e * dn

    @pl.when(e == N_EXPERTS - 1)
    def _():
        o_ref[0] = _layer_norm(alpha * x1_ref[0] + (1.0 + g2_ref[0]) * acc_ref[...], ln_g_ref[...], ln_b_ref[...])


def _moe_call(x1, h2, gates, g2, w_gu, w_dn, ln_g, ln_b, layer, *, tt, alpha):
    bsz, t, d = x1.shape
    r = g2.shape[1]
    tok = lambda n: pl.BlockSpec((1, tt, n), lambda b, i, e: (b, i, 0))
    g2_spec = (pl.BlockSpec((1, 1, d), lambda b, i, e: (b, 0, 0)) if r == 1
               else pl.BlockSpec((1, tt, d), lambda b, i, e: (b, i, 0)))
    vec = pl.BlockSpec((1, d), lambda b, i, e: (0, 0))
    return pl.pallas_call(
        functools.partial(_moe_kernel, alpha=alpha),
        grid=(bsz, t // tt, N_EXPERTS),
        in_specs=[tok(d), tok(d), tok(GATE_LANES), g2_spec,
                  pl.BlockSpec((1, d, 2 * D_EXPERT), lambda b, i, e: (layer * N_EXPERTS + e, 0, 0)),
                  pl.BlockSpec((1, D_EXPERT, d), lambda b, i, e: (layer * N_EXPERTS + e, 0, 0)),
                  vec, vec],
        out_specs=tok(d),
        out_shape=jax.ShapeDtypeStruct((bsz, t, d), F32),
        scratch_shapes=[pltpu.VMEM((tt, d), F32)],
        name="moe_ln",
        compiler_params=_cparams(("arbitrary", "arbitrary", "arbitrary")),
    )(x1, h2, gates, g2, w_gu, w_dn, ln_g, ln_b)


def _rope_tables(pos):
    half = QK_ROPE_DIM // 2
    inv_freq = ROPE_THETA ** (-jnp.arange(half, dtype=F32) / half)
    ang = pos.astype(F32)[:, None] * inv_freq[None, :]
    cos, sin = jnp.cos(ang), jnp.sin(ang)
    t = pos.shape[0]
    tail = HEAD_LANES - ROPE_OFF - QK_ROPE_DIM
    cos_t = jnp.concatenate([jnp.ones((t, ROPE_OFF), F32), cos, cos, jnp.zeros((t, tail), F32)], -1)
    sin_t = jnp.concatenate([jnp.zeros((t, ROPE_OFF), F32), sin, sin, jnp.zeros((t, tail), F32)], -1)
    return cos_t, sin_t


def _rot_partner(w):
    half = QK_ROPE_DIM // 2
    return jnp.concatenate([-w[..., half:], w[..., :half]], -1)


def _head_block(nope, rope):
    lead = nope.shape[:-1] if nope is not None else rope.shape[:-1]
    parts = [nope if nope is not None else jnp.zeros(lead + (QK_NOPE_DIM,), rope.dtype),
             rope if rope is not None else jnp.zeros(lead + (QK_ROPE_DIM,), nope.dtype)]
    parts.append(jnp.zeros(lead + (HEAD_LANES - QK_NOPE_DIM - QK_ROPE_DIM,), parts[0].dtype))
    out = jnp.concatenate(parts, -1)
    return out.reshape(out.shape[:-2] + (out.shape[-2] * HEAD_LANES,))


def _prep_weights(w_in, w_uq, w_uk, w_uv, w_pool, w_o, w_router, w_gate, w_up, w_down):
    depth = w_in.shape[0]
    o_pe = POOL_DIM + Q_LORA_RANK + KV_LORA_RANK
    w_pe = w_in[:, :, o_pe:]
    w_in_ext = jnp.concatenate([w_in[:, :, :o_pe], _head_block(None, w_pe[:, :, None, :]),
                                _head_block(None, _rot_partner(w_pe)[:, :, None, :])], -1).astype(BF16)
    uq = w_uq.reshape(depth, Q_LORA_RANK, N_HEADS, QK_NOPE_DIM + QK_ROPE_DIM)
    w_uq_pad = _head_block(uq[..., :QK_NOPE_DIM], uq[..., QK_NOPE_DIM:]).astype(BF16)
    w_uq_rot = _head_block(None, _rot_partner(uq[..., QK_NOPE_DIM:])).astype(BF16)
    w_uk_pad = _head_block(w_uk, None).astype(BF16)
    w_uv_flat = w_uv.reshape(depth, KV_LORA_RANK, V_LANES).astype(BF16)
    n_g = len(POOL_WINDOWS)
    eye = jnp.eye(n_g, dtype=F32)
    w_pool_bd = (w_pool[:, :, :, None, :] * eye[None, :, None, :, None]).reshape(depth, POOL_DIM, POOL_DIM).astype(BF16)
    r_hi, r_lo = _split_bf16(w_router.T)
    w_r = jnp.concatenate([r_hi, r_lo], 0)
    d = w_gate.shape[2]
    w_gu = jnp.concatenate([w_gate, w_up], -1).astype(BF16).reshape(depth * N_EXPERTS, d, 2 * D_EXPERT)
    w_dn = w_down.astype(BF16).reshape(depth * N_EXPERTS, D_EXPERT, d)
    return w_in_ext, w_uq_pad, w_uq_rot, w_uk_pad, w_uv_flat, w_pool_bd, w_o.astype(BF16), w_r, w_gu, w_dn


def _mod_blocks(mod_rows, d):
    depth, bsz, _ = mod_rows.shape
    m = mod_rows.reshape(depth, bsz, 6, d)
    return jnp.concatenate([m, jnp.zeros((depth, bsz, 2, d), F32)], 2)


def kernel(x_prompt, x_sample, cache_kv, cache_pe, state_pool, c_prompt, c_sample, ln_in_g, ln_in_b, w_ada, b_ada,
           w_in, q_norm_g, kv_norm_g, w_uq, w_uk, w_uv, w_pool, pool_scale, w_o, ln1_g, ln1_b, w_router,
           router_bias, w_gate, w_up, w_down, ln2_g, ln2_b):
    depth = w_in.shape[0]
    n_p, t_p, d = x_prompt.shape
    n_s, t_s, _ = x_sample.shape
    past = cache_kv.shape[2]
    alpha = float((2 * depth) ** 0.25)

    (w_in_ext, w_uq_pad, w_uq_rot, w_uk_pad, w_uv_flat, w_pool_bd, w_o_b, w_r, w_gu, w_dn) = _prep_weights(
        w_in, w_uq, w_uk, w_uv, w_pool, w_o, w_router, w_gate, w_up, w_down)
    row = lambda a: a.reshape(1, -1)
    rb_t = router_bias.reshape(N_EXPERTS, 1)

    n_c = n_p + n_s
    rows = -(-n_c // 8) * 8
    c_all = jnp.concatenate([c_prompt, c_sample, jnp.zeros((rows - n_c, d), F32)], 0)
    mod = _ada_call(c_all, w_ada, b_ada)
    mod_p = _mod_blocks(mod[:, :n_p], d)
    mod_s = _mod_blocks(mod[:, n_p:n_c], d)

    cos_p, sin_p = _rope_tables(jnp.arange(t_p, dtype=jnp.int32))
    cos_s, sin_s = _rope_tables(past + jnp.arange(t_s, dtype=jnp.int32))
    cache_pe_blk = _head_block(None, cache_pe[:, :, :, None, :])
    left_p = jnp.zeros((depth, n_p, HALO, POOL_DIM), F32)
    left_s = jnp.concatenate([jnp.zeros((depth, n_s, HALO - POOL_CTX, POOL_DIM), F32), state_pool], 2)

    tt_p = min(512, t_p)
    tt_s = t_s

    def run(x, mod_g, cos, sin, left, tt, pos0, prompt):
        kvs, pes, pools = [], [], []
        for l in range(depth):
            w_pre = (w_in_ext[l], row(q_norm_g[l]), row(kv_norm_g[l]), w_uq_pad[l], w_uq_rot[l], w_uk_pad[l],
                     w_uv_flat[l])
            outs = _pre_call(x, mod_g[l], (row(ln_in_g), row(ln_in_b)), w_pre, cos, sin, first=(l == 0), tt=tt)
            if l == 0:
                x = outs[0]
                outs = outs[1:]
            u, ckv, kpe, q, k, v = outs
            if prompt:
                att = _prompt_attn_call(q, k, v, tq=tt)
            else:
                att = _sample_attn_call(q, k, v, cache_kv, cache_pe_blk, w_uk_pad[l], w_uv_flat[l], l)
            w_post = (w_pool_bd[l], row(pool_scale[l]), w_o_b[l], row(ln1_g[l]), row(ln1_b[l]), w_r, rb_t)
            x1, h2, gates = _post_call(x, u, left[l], att, mod_g[l], w_post, tt=tt, pos0=pos0, alpha=alpha)
            g2 = mod_g[l][:, 5:6, :]
            if prompt:
                x = _moe_call(x1, h2, gates, g2, w_gu, w_dn, row(ln2_g[l]), row(ln2_b[l]), l, tt=tt, alpha=alpha)
            else:
                bsz, t, _ = x1.shape
                flat = lambda a: a.reshape(1, bsz * t, a.shape[-1])
                g2_tok = jnp.broadcast_to(g2, (bsz, t, d))
                x = _moe_call(flat(x1), flat(h2), flat(gates), flat(g2_tok), w_gu, w_dn, row(ln2_g[l]),
                              row(ln2_b[l]), l, tt=bsz * t, alpha=alpha).reshape(bsz, t, d)
            kvs.append(ckv)
            pes.append(kpe[:, :, ROPE_OFF:ROPE_OFF + QK_ROPE_DIM])
            pools.append(u[:, -POOL_CTX:, :])
        return x, jnp.stack(kvs), jnp.stack(pes), jnp.stack(pools)

    y_p, kv_p, pe_p, pool_p = run(x_prompt, mod_p, cos_p, sin_p, left_p, tt_p, 0, True)
    y_s, kv_s, pe_s, pool_s = run(x_sample, mod_s, cos_s, sin_s, left_s, tt_s, past, False)
    return (y_p, y_s, kv_p, pe_p, pool_p, kv_s, pe_s, pool_s)
```

```python
import functools

import jax
import jax.numpy as jnp
from jax import lax
from jax.experimental import pallas as pl
from jax.experimental.pallas import tpu as pltpu

F32 = jnp.float32
BF16 = jnp.bfloat16

CHUNK = 64
POOL_WINDOWS = (2, 4, 8, 16)
POOL_DIM = 256
POOL_GROUP_DIM = 64
POOL_CTX = 15
HALO = 16
QK_NOPE_DIM = 64
QK_ROPE_DIM = 32
V_HEAD_DIM = 64
N_HEADS = 12
Q_LORA_RANK = 384
KV_LORA_RANK = 256
ROPE_THETA = 10000.0
SOFTMAX_SCALE = (QK_NOPE_DIM + QK_ROPE_DIM) ** -0.5
N_EXPERTS = 16
N_EXPERT_GROUPS = 4
EXPERTS_PER_GROUP = 4
D_EXPERT = 512
LN_EPS = 1e-5
RMS_EPS = 1e-6

HEAD_LANES = 128
ROPE_OFF = QK_NOPE_DIM
QK_LANES = N_HEADS * HEAD_LANES
V_LANES = N_HEADS * V_HEAD_DIM
VA_LANES = N_HEADS * HEAD_LANES
ONES_LANE = (V_HEAD_DIM, 0)
LOG2_E = 1.4426950408889634
Z_LANES = POOL_DIM + Q_LORA_RANK + KV_LORA_RANK + 2 * HEAD_LANES
GATE_LANES = 128

VMEM_LIMIT = 56 * 1024 * 1024
ATTN_TILE = 1024


def _cparams(sem):
    return pltpu.CompilerParams(dimension_semantics=sem, vmem_limit_bytes=VMEM_LIMIT)


def _layer_norm(x, g, b):
    mu = jnp.mean(x, -1, keepdims=True)
    xc = x - mu
    var = jnp.mean(xc * xc, -1, keepdims=True)
    return xc * lax.rsqrt(var + LN_EPS) * g + b


def _rms_norm(x, g):
    return x * lax.rsqrt(jnp.mean(x * x, -1, keepdims=True) + RMS_EPS) * g


def _dot(a, b):
    return jnp.dot(a, b, preferred_element_type=F32)


def _dot_nt(a, b):
    return lax.dot_general(a, b, (((1,), (1,)), ((), ())), preferred_element_type=F32)


def _split_bf16(x):
    hi = x.astype(BF16)
    lo = (x - hi.astype(F32)).astype(BF16)
    return hi, lo


def _ada_kernel(c_ref, w_ref, b_ref, o_ref):
    c = c_ref[...]
    s = c * jax.nn.sigmoid(c)
    s_hi, s_lo = _split_bf16(s)
    w_hi, w_lo = _split_bf16(w_ref[0])
    o_ref[0] = _dot(s_hi, w_hi) + _dot(s_lo, w_hi) + _dot(s_hi, w_lo) + b_ref[0]


def _ada_call(c_all, w_ada, b_ada):
    depth, d, n = w_ada.shape
    rows = c_all.shape[0]
    tn = 1536
    return pl.pallas_call(
        _ada_kernel,
        grid=(depth, n // tn),
        in_specs=[pl.BlockSpec((rows, d), lambda l, j: (0, 0)),
                  pl.BlockSpec((1, d, tn), lambda l, j: (l, 0, j)),
                  pl.BlockSpec((1, 1, tn), lambda l, j: (l, 0, j))],
        out_specs=pl.BlockSpec((1, rows, tn), lambda l, j: (l, 0, j)),
        out_shape=jax.ShapeDtypeStruct((depth, rows, n), F32),
        name="ada_mod",
        compiler_params=_cparams(("arbitrary", "arbitrary")),
    )(c_all, w_ada, b_ada.reshape(depth, 1, n))


def _pre_kernel(*refs, first):
    if first:
        (x_ref, mod_ref, lng_ref, lnb_ref, w_in_ref, qg_ref, kvg_ref, w_uq_ref, w_uqr_ref, w_uk_ref, w_uv_ref,
         ones_ref, cos_ref, sin_ref, xn_ref, u_ref, ckv_ref, kpe_ref, q_ref, k_ref, v_ref) = refs
    else:
        (x_ref, mod_ref, w_in_ref, qg_ref, kvg_ref, w_uq_ref, w_uqr_ref, w_uk_ref, w_uv_ref,
         ones_ref, cos_ref, sin_ref, u_ref, ckv_ref, kpe_ref, q_ref, k_ref, v_ref) = refs
    x = x_ref[0]
    if first:
        x = _layer_norm(x, lng_ref[...], lnb_ref[...])
        xn_ref[0] = x
    sh1 = mod_ref[0, 0:1, :]
    sc1 = mod_ref[0, 1:2, :]
    h = (x * (1.0 + sc1) + sh1).astype(BF16)
    z = _dot(h, w_in_ref[...])
    o_cq = POOL_DIM
    o_kv = o_cq + Q_LORA_RANK
    o_pe = o_kv + KV_LORA_RANK
    u_ref[0] = z[:, :POOL_DIM]
    cqn = _rms_norm(z[:, o_cq:o_kv], qg_ref[...])
    ckv = _rms_norm(z[:, o_kv:o_pe], kvg_ref[...])
    ckv_ref[0] = ckv
    cos = cos_ref[...]
    sin = sin_ref[...]
    kpe = z[:, o_pe:o_pe + HEAD_LANES] * cos + z[:, o_pe + HEAD_LANES:] * sin
    kpe_ref[0] = kpe
    cqb = cqn.astype(BF16)
    ckvb = ckv.astype(BF16)
    qa = _dot(cqb, w_uq_ref[...])
    qb = _dot(cqb, w_uqr_ref[...])
    kn = _dot(ckvb, w_uk_ref[...])
    v_ref[0] = (_dot(ckvb, w_uv_ref[...]) + ones_ref[...]).astype(BF16)
    for hd in range(N_HEADS):
        sl = slice(hd * HEAD_LANES, (hd + 1) * HEAD_LANES)
        q_ref[0, :, sl] = ((qa[:, sl] * cos + qb[:, sl] * sin) * (SOFTMAX_SCALE * LOG2_E)).astype(BF16)
        k_ref[0, :, sl] = (kn[:, sl] + kpe).astype(BF16)


def _pre_call(x, mod, ln_in, w, cos, sin, *, first, tt):
    bsz, t, d = x.shape
    grid = (bsz, t // tt)
    tok = lambda n: pl.BlockSpec((1, tt, n), lambda b, i: (b, i, 0))
    full = lambda a: pl.BlockSpec(a.shape, lambda b, i: (0,) * a.ndim)
    in_arrays = [x, mod]
    in_specs = [tok(d), pl.BlockSpec((1, 8, d), lambda b, i: (b, 0, 0))]
    if first:
        in_arrays += list(ln_in)
        in_specs += [full(a) for a in ln_in]
    in_arrays += list(w)
    in_specs += [full(a) for a in w]
    in_arrays += [cos, sin]
    in_specs += [pl.BlockSpec((tt, HEAD_LANES), lambda b, i: (i, 0))] * 2
    out_shape, out_specs = [], []
    if first:
        out_shape.append(jax.ShapeDtypeStruct((bsz, t, d), F32))
        out_specs.append(tok(d))
    for n, dt in ((POOL_DIM, F32), (KV_LORA_RANK, F32), (HEAD_LANES, F32), (QK_LANES, BF16), (QK_LANES, BF16),
                  (VA_LANES, BF16)):
        out_shape.append(jax.ShapeDtypeStruct((bsz, t, n), dt))
        out_specs.append(tok(n))
    return pl.pallas_call(
        functools.partial(_pre_kernel, first=first),
        grid=grid, in_specs=in_specs, out_specs=out_specs, out_shape=out_shape,
        name="pre_mixer",
        compiler_params=_cparams(("arbitrary", "arbitrary")),
    )(*in_arrays)


def _pair_output(accs):
    outs = [accs[hh] / accs[hh][:, ONES_LANE[hh]:ONES_LANE[hh] + 1] for hh in range(2)]
    lane = lax.broadcasted_iota(jnp.int32, outs[0].shape, 1)
    return jnp.where(lane < V_HEAD_DIM, outs[0], outs[1])


def _prompt_attn_kernel(q_ref, k_ref, v_ref, o_ref, *, tq):
    qi = pl.program_id(2)
    heads = [slice(hh * HEAD_LANES, (hh + 1) * HEAD_LANES) for hh in range(2)]
    qs = [q_ref[0, :, hs] for hs in heads]

    def step(j, carry, mask):
        start = pl.multiple_of(j * tq, tq)
        new = []
        for hh, hs in enumerate(heads):
            m, acc = carry[hh]
            s = _dot_nt(qs[hh], k_ref[0, pl.ds(start, tq), hs])
            if mask is not None:
                s = jnp.where(mask, s, -jnp.inf)
            m_new = jnp.maximum(m, jnp.max(s, -1, keepdims=True))
            p = jnp.exp2(s - m_new)
            acc = jnp.exp2(m - m_new) * acc + _dot(p.astype(BF16), v_ref[0, pl.ds(start, tq), hs])
            new.append((m_new, acc))
        return tuple(new)

    init = tuple((jnp.full((tq, 1), -jnp.inf, F32), jnp.zeros((tq, HEAD_LANES), F32)) for _ in heads)
    carry = lax.fori_loop(0, qi, lambda j, c: step(j, c, None), init)
    row_chunk = lax.broadcasted_iota(jnp.int32, (tq, tq), 0) // CHUNK
    col_chunk = lax.broadcasted_iota(jnp.int32, (tq, tq), 1) // CHUNK
    carry = step(qi, carry, col_chunk <= row_chunk)
    o_ref[0] = _pair_output([carry[hh][1] for hh in range(2)]).astype(o_ref.dtype)


def _prompt_attn_call(q, k, v, *, tq):
    bsz, t, _ = q.shape
    pair = 2 * HEAD_LANES
    return pl.pallas_call(
        functools.partial(_prompt_attn_kernel, tq=tq),
        grid=(bsz, N_HEADS // 2, t // tq),
        in_specs=[pl.BlockSpec((1, tq, pair), lambda b, hp, i: (b, i, hp)),
                  pl.BlockSpec((1, t, pair), lambda b, hp, i: (b, 0, hp)),
                  pl.BlockSpec((1, t, pair), lambda b, hp, i: (b, 0, hp))],
        out_specs=pl.BlockSpec((1, tq, 2 * V_HEAD_DIM), lambda b, hp, i: (b, i, hp)),
        out_shape=jax.ShapeDtypeStruct((bsz, t, V_LANES), BF16),
        name="attn_prompt",
        compiler_params=_cparams(("arbitrary", "arbitrary", "arbitrary")),
    )(q, k, v)


def _sample_attn_kernel(q_ref, kn_ref, vn_ref, ckv_ref, cpe_ref, w_uk_ref, w_uv_ref, ones_ref, o_ref):
    ckvb = ckv_ref[0, 0].astype(BF16)
    cpe = cpe_ref[0, 0]
    for hp in range(N_HEADS // 2):
        accs = []
        for hh in range(2):
            hd = 2 * hp + hh
            hs = slice(hd * HEAD_LANES, (hd + 1) * HEAD_LANES)
            q = q_ref[0, :, hs]
            k_c = (_dot(ckvb, w_uk_ref[:, hs]) + cpe).astype(BF16)
            v_c = (_dot(ckvb, w_uv_ref[:, hs]) + ones_ref[:, hs]).astype(BF16)
            s_c = _dot_nt(q, k_c)
            s_n = _dot_nt(q, kn_ref[0, :, hs])
            m = jnp.maximum(jnp.max(s_c, -1, keepdims=True), jnp.max(s_n, -1, keepdims=True))
            p_c = jnp.exp2(s_c - m).astype(BF16)
            p_n = jnp.exp2(s_n - m).astype(BF16)
            accs.append(_dot(p_c, v_c) + _dot(p_n, vn_ref[0, :, hs]))
        vs = slice(hp * 2 * V_HEAD_DIM, (hp + 1) * 2 * V_HEAD_DIM)
        o_ref[0, :, vs] = _pair_output(accs).astype(o_ref.dtype)


def _sample_attn_call(q, kn, vn, cache_kv, cache_pe, w_uk, w_uv, ones_row, layer):
    bsz, t, _ = q.shape
    past = cache_kv.shape[2]
    tok = lambda n: pl.BlockSpec((1, t, n), lambda b: (b, 0, 0))
    full = lambda a: pl.BlockSpec(a.shape, lambda b: (0,) * a.ndim)
    return pl.pallas_call(
        _sample_attn_kernel,
        grid=(bsz,),
        in_specs=[tok(QK_LANES), tok(QK_LANES), tok(VA_LANES),
                  pl.BlockSpec((1, 1, past, KV_LORA_RANK), lambda b: (layer, b, 0, 0)),
                  pl.BlockSpec((1, 1, past, HEAD_LANES), lambda b: (layer, b, 0, 0)),
                  full(w_uk), full(w_uv), full(ones_row)],
        out_specs=tok(V_LANES),
        out_shape=jax.ShapeDtypeStruct((bsz, t, V_LANES), BF16),
        name="attn_sample",
        compiler_params=_cparams(("arbitrary",)),
    )(q, kn, vn, cache_kv, cache_pe, w_uk, w_uv, ones_row)


def _route_gates(logits_t, bias_t):
    score = [jax.nn.sigmoid(logits_t[e:e + 1, :]) for e in range(N_EXPERTS)]
    sel = [score[e] + bias_t[e:e + 1, :] for e in range(N_EXPERTS)]
    grp = []
    for g in range(N_EXPERT_GROUPS):
        mem = sel[g * EXPERTS_PER_GROUP:(g + 1) * EXPERTS_PER_GROUP]
        best = None
        for a in range(EXPERTS_PER_GROUP):
            for b in range(a + 1, EXPERTS_PER_GROUP):
                pair = mem[a] + mem[b]
                best = pair if best is None else jnp.maximum(best, pair)
        grp.append(best)
    in_best = []
    for g in range(N_EXPERT_GROUPS):
        ok = None
        for g2 in range(N_EXPERT_GROUPS):
            if g2 == g:
                continue
            c = (grp[g] > grp[g2]) if g2 < g else (grp[g] >= grp[g2])
            ok = c if ok is None else (ok & c)
        in_best.append(ok)
    picked = []
    for e in range(N_EXPERTS):
        g = e // EXPERTS_PER_GROUP
        rank = jnp.zeros_like(sel[e], dtype=jnp.int32)
        for e2 in range(g * EXPERTS_PER_GROUP, (g + 1) * EXPERTS_PER_GROUP):
            if e2 == e:
                continue
            ahead = (sel[e2] >= sel[e]) if e2 < e else (sel[e2] > sel[e])
            rank = rank + ahead.astype(jnp.int32)
        picked.append(in_best[g] & (rank < 2))
    wsel = [jnp.where(picked[e], score[e], 0.0) for e in range(N_EXPERTS)]
    total = wsel[0]
    for e in range(1, N_EXPERTS):
        total = total + wsel[e]
    return [wsel[e] / total for e in range(N_EXPERTS)]


def _post_kernel(x_ref, u_ref, halo_ref, left_ref, att_ref, mod_ref, w_pool_ref, pscale_ref, w_o_ref,
                 ln_g_ref, ln_b_ref, w_r_ref, rb_ref, x1_ref, h2_ref, gate_ref, ext_ref, *, tt, pos0, alpha):
    i = pl.program_id(1)
    u = u_ref[0]
    halo = jnp.where(i == 0, left_ref[0], halo_ref[0])
    ext_ref[0:HALO, :] = halo
    ext_ref[HALO:HALO + tt, :] = u
    sums = {}
    run = u
    for j in range(1, max(POOL_WINDOWS)):
        run = run + ext_ref[HALO - j:HALO - j + tt, :]
        if j + 1 in POOL_WINDOWS:
            sums[j + 1] = run
    lane = lax.broadcasted_iota(jnp.int32, (tt, POOL_DIM), 1)
    grp = lane // POOL_GROUP_DIM
    win_sum = sums[POOL_WINDOWS[-1]]
    win_len = jnp.full((tt, POOL_DIM), POOL_WINDOWS[-1], jnp.int32)
    for g in range(len(POOL_WINDOWS) - 2, -1, -1):
        win_sum = jnp.where(grp == g, sums[POOL_WINDOWS[g]], win_sum)
        win_len = jnp.where(grp == g, POOL_WINDOWS[g], win_len)
    pos = pos0 + i * tt + lax.broadcasted_iota(jnp.int32, (tt, POOL_DIM), 0)
    cnt = jnp.minimum(win_len, pos + 1).astype(F32)
    pooled = (win_sum / cnt - u).astype(BF16)
    y_pool = _dot(pooled, w_pool_ref[...]) * pscale_ref[...]
    y = _dot(y_pool.astype(BF16), w_o_ref[0:POOL_DIM, :]) + _dot(att_ref[0], w_o_ref[POOL_DIM:, :])
    g1 = mod_ref[0, 2:3, :]
    sh2 = mod_ref[0, 3:4, :]
    sc2 = mod_ref[0, 4:5, :]
    x1 = _layer_norm(alpha * x_ref[0] + (1.0 + g1) * y, ln_g_ref[...], ln_b_ref[...])
    x1_ref[0] = x1
    h2 = x1 * (1.0 + sc2) + sh2
    h2_ref[0] = h2.astype(BF16)
    h_hi, h_lo = _split_bf16(h2)
    w_r = w_r_ref[...]
    l_hi = _dot_nt(w_r, h_hi)
    l_lo = _dot_nt(w_r[0:N_EXPERTS, :], h_lo)
    logits_t = l_hi[0:N_EXPERTS, :] + l_hi[N_EXPERTS:, :] + l_lo
    rows = _route_gates(logits_t, rb_ref[...])
    rows.append(jnp.zeros((GATE_LANES - N_EXPERTS, tt), F32))
    gate_ref[0] = jnp.concatenate(rows, axis=0).T


def _post_call(x, u, left, att, mod, w, *, tt, pos0, alpha):
    bsz, t, d = x.shape
    per_tile = tt // HALO
    tok = lambda n: pl.BlockSpec((1, tt, n), lambda b, i: (b, i, 0))
    full = lambda a: pl.BlockSpec(a.shape, lambda b, i: (0,) * a.ndim)
    in_specs = [tok(d), tok(POOL_DIM),
                pl.BlockSpec((1, HALO, POOL_DIM), lambda b, i: (b, jnp.maximum(i * per_tile - 1, 0), 0)),
                pl.BlockSpec((1, HALO, POOL_DIM), lambda b, i: (b, 0, 0)),
                tok(V_LANES), pl.BlockSpec((1, 8, d), lambda b, i: (b, 0, 0))] + [full(a) for a in w]
    out_shape = [jax.ShapeDtypeStruct((bsz, t, d), F32), jax.ShapeDtypeStruct((bsz, t, d), BF16),
                 jax.ShapeDtypeStruct((bsz, t, GATE_LANES), F32)]
    out_specs = [tok(d), tok(d), tok(GATE_LANES)]
    return pl.pallas_call(
        functools.partial(_post_kernel, tt=tt, pos0=pos0, alpha=alpha),
        grid=(bsz, t // tt), in_specs=in_specs, out_specs=out_specs, out_shape=out_shape,
        scratch_shapes=[pltpu.VMEM((HALO + tt, POOL_DIM), F32)],
        name="post_mixer",
        compiler_params=_cparams(("arbitrary", "arbitrary")),
    )(x, u, u, left, att, mod, *w)


def _moe_kernel(x1_ref, h2_ref, gate_ref, g2_ref, w_gu_ref, w_dn_ref, ln_g_ref, ln_b_ref, o_ref, acc_ref, *, alpha):
    e = pl.program_id(2)

    @pl.when(e == 0)
    def _():
        acc_ref[...] = jnp.zeros_like(acc_ref)

    gu = _dot(h2_ref[0], w_gu_ref[0])
    gate_part = gu[:, :D_EXPERT]
    a = gate_part * jax.nn.sigmoid(gate_part) * gu[:, D_EXPERT:]
    dn = _dot(a.astype(BF16), w_dn_ref[0])
    gates = gate_ref[0]
    lane = lax.broadcasted_iota(jnp.int32, gates.shape, 1)
    g_e = jnp.sum(jnp.where(lane == e, gates, 0.0), -1, keepdims=True)
    acc_ref[...] += g_e * dn

    @pl.when(e == N_EXPERTS - 1)
    def _():
        o_ref[0] = _layer_norm(alpha * x1_ref[0] + (1.0 + g2_ref[0]) * acc_ref[...], ln_g_ref[...], ln_b_ref[...])


def _moe_call(x1, h2, gates, g2, w_gu, w_dn, ln_g, ln_b, layer, *, tt, alpha):
    bsz, t, d = x1.shape
    r = g2.shape[1]
    tok = lambda n: pl.BlockSpec((1, tt, n), lambda b, i, e: (b, i, 0))
    g2_spec = (pl.BlockSpec((1, 1, d), lambda b, i, e: (b, 0, 0)) if r == 1
               else pl.BlockSpec((1, tt, d), lambda b, i, e: (b, i, 0)))
    vec = pl.BlockSpec((1, d), lambda b, i, e: (0, 0))
    return pl.pallas_call(
        functools.partial(_moe_kernel, alpha=alpha),
        grid=(bsz, t // tt, N_EXPERTS),
        in_specs=[tok(d), tok(d), tok(GATE_LANES), g2_spec,
                  pl.BlockSpec((1, d, 2 * D_EXPERT), lambda b, i, e: (layer * N_EXPERTS + e, 0, 0)),
                  pl.BlockSpec((1, D_EXPERT, d), lambda b, i, e: (layer * N_EXPERTS + e, 0, 0)),
                  vec, vec],
        out_specs=tok(d),
        out_shape=jax.ShapeDtypeStruct((bsz, t, d), F32),
        scratch_shapes=[pltpu.VMEM((tt, d), F32)],
        name="moe_ln",
        compiler_params=_cparams(("arbitrary", "arbitrary", "arbitrary")),
    )(x1, h2, gates, g2, w_gu, w_dn, ln_g, ln_b)


def _rope_tables(pos):
    half = QK_ROPE_DIM // 2
    inv_freq = ROPE_THETA ** (-jnp.arange(half, dtype=F32) / half)
    ang = pos.astype(F32)[:, None] * inv_freq[None, :]
    cos, sin = jnp.cos(ang), jnp.sin(ang)
    t = pos.shape[0]
    tail = HEAD_LANES - ROPE_OFF - QK_ROPE_DIM
    cos_t = jnp.concatenate([jnp.ones((t, ROPE_OFF), F32), cos, cos, jnp.zeros((t, tail), F32)], -1)
    sin_t = jnp.concatenate([jnp.zeros((t, ROPE_OFF), F32), sin, sin, jnp.zeros((t, tail), F32)], -1)
    return cos_t, sin_t


def _rot_partner(w):
    half = QK_ROPE_DIM // 2
    return jnp.concatenate([-w[..., half:], w[..., :half]], -1)


def _head_block(nope, rope):
    lead = nope.shape[:-1] if nope is not None else rope.shape[:-1]
    parts = [nope if nope is not None else jnp.zeros(lead + (QK_NOPE_DIM,), rope.dtype),
             rope if rope is not None else jnp.zeros(lead + (QK_ROPE_DIM,), nope.dtype)]
    parts.append(jnp.zeros(lead + (HEAD_LANES - QK_NOPE_DIM - QK_ROPE_DIM,), parts[0].dtype))
    out = jnp.concatenate(parts, -1)
    return out.reshape(out.shape[:-2] + (out.shape[-2] * HEAD_LANES,))


def _prep_weights(w_in, w_uq, w_uk, w_uv, w_pool, w_o, w_router, w_gate, w_up, w_down):
    depth = w_in.shape[0]
    o_pe = POOL_DIM + Q_LORA_RANK + KV_LORA_RANK
    w_pe = w_in[:, :, o_pe:]
    w_in_ext = jnp.concatenate([w_in[:, :, :o_pe], _head_block(None, w_pe[:, :, None, :]),
                                _head_block(None, _rot_partner(w_pe)[:, :, None, :])], -1).astype(BF16)
    uq = w_uq.reshape(depth, Q_LORA_RANK, N_HEADS, QK_NOPE_DIM + QK_ROPE_DIM)
    w_uq_pad = _head_block(uq[..., :QK_NOPE_DIM], uq[..., QK_NOPE_DIM:]).astype(BF16)
    w_uq_rot = _head_block(None, _rot_partner(uq[..., QK_NOPE_DIM:])).astype(BF16)
    w_uk_pad = _head_block(w_uk, None).astype(BF16)
    pad = jnp.zeros(w_uv.shape[:2] + (N_HEADS // 2, HEAD_LANES - V_HEAD_DIM), w_uv.dtype)
    uv = w_uv.reshape(depth, KV_LORA_RANK, N_HEADS // 2, 2, V_HEAD_DIM)
    w_uv_flat = jnp.concatenate([uv[:, :, :, 0], pad, pad, uv[:, :, :, 1]], -1).reshape(
        depth, KV_LORA_RANK, VA_LANES).astype(BF16)
    n_g = len(POOL_WINDOWS)
    eye = jnp.eye(n_g, dtype=F32)
    w_pool_bd = (w_pool[:, :, :, None, :] * eye[None, :, None, :, None]).reshape(depth, POOL_DIM, POOL_DIM).astype(BF16)
    r_hi, r_lo = _split_bf16(w_router.T)
    w_r = jnp.concatenate([r_hi, r_lo], 0)
    d = w_gate.shape[2]
    w_gu = jnp.concatenate([w_gate, w_up], -1).astype(BF16).reshape(depth * N_EXPERTS, d, 2 * D_EXPERT)
    w_dn = w_down.astype(BF16).reshape(depth * N_EXPERTS, D_EXPERT, d)
    return w_in_ext, w_uq_pad, w_uq_rot, w_uk_pad, w_uv_flat, w_pool_bd, w_o.astype(BF16), w_r, w_gu, w_dn


def _mod_blocks(mod_rows, d):
    depth, bsz, _ = mod_rows.shape
    m = mod_rows.reshape(depth, bsz, 6, d)
    return jnp.concatenate([m, jnp.zeros((depth, bsz, 2, d), F32)], 2)


def kernel(x_prompt, x_sample, cache_kv, cache_pe, state_pool, c_prompt, c_sample, ln_in_g, ln_in_b, w_ada, b_ada,
           w_in, q_norm_g, kv_norm_g, w_uq, w_uk, w_uv, w_pool, pool_scale, w_o, ln1_g, ln1_b, w_router,
           router_bias, w_gate, w_up, w_down, ln2_g, ln2_b):
    depth = w_in.shape[0]
    n_p, t_p, d = x_prompt.shape
    n_s, t_s, _ = x_sample.shape
    past = cache_kv.shape[2]
    alpha = float((2 * depth) ** 0.25)

    (w_in_ext, w_uq_pad, w_uq_rot, w_uk_pad, w_uv_flat, w_pool_bd, w_o_b, w_r, w_gu, w_dn) = _prep_weights(
        w_in, w_uq, w_uk, w_uv, w_pool, w_o, w_router, w_gate, w_up, w_down)
    row = lambda a: a.reshape(1, -1)
    rb_t = router_bias.reshape(N_EXPERTS, 1)
    head_lane = jnp.arange(VA_LANES, dtype=jnp.int32) % HEAD_LANES
    odd_head = (jnp.arange(VA_LANES, dtype=jnp.int32) // HEAD_LANES) % 2 == 1
    ones_row = (head_lane == jnp.where(odd_head, ONES_LANE[1], ONES_LANE[0])).astype(F32).reshape(1, VA_LANES)

    n_c = n_p + n_s
    rows = -(-n_c // 8) * 8
    c_all = jnp.concatenate([c_prompt, c_sample, jnp.zeros((rows - n_c, d), F32)], 0)
    mod = _ada_call(c_all, w_ada, b_ada)
    mod_p = _mod_blocks(mod[:, :n_p], d)
    mod_s = _mod_blocks(mod[:, n_p:n_c], d)

    cos_p, sin_p = _rope_tables(jnp.arange(t_p, dtype=jnp.int32))
    cos_s, sin_s = _rope_tables(past + jnp.arange(t_s, dtype=jnp.int32))
    cache_pe_blk = _head_block(None, cache_pe[:, :, :, None, :])
    left_p = jnp.zeros((depth, n_p, HALO, POOL_DIM), F32)
    left_s = jnp.concatenate([jnp.zeros((depth, n_s, HALO - POOL_CTX, POOL_DIM), F32), state_pool], 2)

    tt_p = min(512, t_p)
    tt_s = t_s

    def run(x, mod_g, cos, sin, left, tt, pos0, prompt):
        kvs, pes, pools = [], [], []
        for l in range(depth):
            w_pre = (w_in_ext[l], row(q_norm_g[l]), row(kv_norm_g[l]), w_uq_pad[l], w_uq_rot[l], w_uk_pad[l],
                     w_uv_flat[l], ones_row)
            outs = _pre_call(x, mod_g[l], (row(ln_in_g), row(ln_in_b)), w_pre, cos, sin, first=(l == 0), tt=tt)
            if l == 0:
                x = outs[0]
                outs = outs[1:]
            u, ckv, kpe, q, k, v = outs
            if prompt:
                att = _prompt_attn_call(q, k, v, tq=min(ATTN_TILE, x.shape[1]))
            else:
                att = _sample_attn_call(q, k, v, cache_kv, cache_pe_blk, w_uk_pad[l], w_uv_flat[l], ones_row, l)
            w_post = (w_pool_bd[l], row(pool_scale[l]), w_o_b[l], row(ln1_g[l]), row(ln1_b[l]), w_r, rb_t)
            x1, h2, gates = _post_call(x, u, left[l], att, mod_g[l], w_post, tt=tt, pos0=pos0, alpha=alpha)
            g2 = mod_g[l][:, 5:6, :]
            if prompt:
                x = _moe_call(x1, h2, gates, g2, w_gu, w_dn, row(ln2_g[l]), row(ln2_b[l]), l, tt=tt, alpha=alpha)
            else:
                bsz, t, _ = x1.shape
                flat = lambda a: a.reshape(1, bsz * t, a.shape[-1])
                g2_tok = jnp.broadcast_to(g2, (bsz, t, d))
                x = _moe_call(flat(x1), flat(h2), flat(gates), flat(g2_tok), w_gu, w_dn, row(ln2_g[l]),
                              row(ln2_b[l]), l, tt=bsz * t, alpha=alpha).reshape(bsz, t, d)
            kvs.append(ckv)
            pes.append(kpe[:, :, ROPE_OFF:ROPE_OFF + QK_ROPE_DIM])
            pools.append(u[:, -POOL_CTX:, :])
        return x, jnp.stack(kvs), jnp.stack(pes), jnp.stack(pools)

    y_p, kv_p, pe_p, pool_p = run(x_prompt, mod_p, cos_p, sin_p, left_p, tt_p, 0, True)
    y_s, kv_s, pe_s, pool_s = run(x_sample, mod_s, cos_s, sin_s, left_s, tt_s, past, False)
    return (y_p, y_s, kv_p, pe_p, pool_p, kv_s, pe_s, pool_s)
```

```python
import functools

import jax
import jax.numpy as jnp
from jax import lax
from jax.experimental import pallas as pl
from jax.experimental.pallas import tpu as pltpu

F32 = jnp.float32
BF16 = jnp.bfloat16

CHUNK = 64
POOL_WINDOWS = (2, 4, 8, 16)
POOL_DIM = 256
POOL_GROUP_DIM = 64
POOL_CTX = 15
HALO = 16
QK_NOPE_DIM = 64
QK_ROPE_DIM = 32
V_HEAD_DIM = 64
N_HEADS = 12
Q_LORA_RANK = 384
KV_LORA_RANK = 256
ROPE_THETA = 10000.0
SOFTMAX_SCALE = (QK_NOPE_DIM + QK_ROPE_DIM) ** -0.5
N_EXPERTS = 16
N_EXPERT_GROUPS = 4
EXPERTS_PER_GROUP = 4
D_EXPERT = 512
LN_EPS = 1e-5
RMS_EPS = 1e-6

HEAD_LANES = 128
ROPE_OFF = QK_NOPE_DIM
QK_LANES = N_HEADS * HEAD_LANES
V_LANES = N_HEADS * V_HEAD_DIM
VA_LANES = N_HEADS * HEAD_LANES
ONES_LANE = (V_HEAD_DIM, 0)
LOG2_E = 1.4426950408889634
Z_LANES = POOL_DIM + Q_LORA_RANK + KV_LORA_RANK + 2 * HEAD_LANES
GATE_LANES = 128

VMEM_LIMIT = 56 * 1024 * 1024
ATTN_TILE = 1024
MOE_TILE = 1024


def _cparams(sem):
    return pltpu.CompilerParams(dimension_semantics=sem, vmem_limit_bytes=VMEM_LIMIT)


def _layer_norm(x, g, b):
    mu = jnp.mean(x, -1, keepdims=True)
    xc = x - mu
    var = jnp.mean(xc * xc, -1, keepdims=True)
    return xc * lax.rsqrt(var + LN_EPS) * g + b


def _rms_norm(x, g):
    return x * lax.rsqrt(jnp.mean(x * x, -1, keepdims=True) + RMS_EPS) * g


def _dot(a, b):
    return jnp.dot(a, b, preferred_element_type=F32)


def _dot_nt(a, b):
    return lax.dot_general(a, b, (((1,), (1,)), ((), ())), preferred_element_type=F32)


def _split_bf16(x):
    hi = x.astype(BF16)
    lo = (x - hi.astype(F32)).astype(BF16)
    return hi, lo


def _ada_kernel(c_ref, w_ref, b_ref, o_ref):
    c = c_ref[...]
    s = c * jax.nn.sigmoid(c)
    s_hi, s_lo = _split_bf16(s)
    w_hi, w_lo = _split_bf16(w_ref[0])
    o_ref[0] = _dot(s_hi, w_hi) + _dot(s_lo, w_hi) + _dot(s_hi, w_lo) + b_ref[0]


def _ada_call(c_all, w_ada, b_ada):
    depth, d, n = w_ada.shape
    rows = c_all.shape[0]
    tn = 1536
    return pl.pallas_call(
        _ada_kernel,
        grid=(depth, n // tn),
        in_specs=[pl.BlockSpec((rows, d), lambda l, j: (0, 0)),
                  pl.BlockSpec((1, d, tn), lambda l, j: (l, 0, j)),
                  pl.BlockSpec((1, 1, tn), lambda l, j: (l, 0, j))],
        out_specs=pl.BlockSpec((1, rows, tn), lambda l, j: (l, 0, j)),
        out_shape=jax.ShapeDtypeStruct((depth, rows, n), F32),
        name="ada_mod",
        compiler_params=_cparams(("arbitrary", "arbitrary")),
    )(c_all, w_ada, b_ada.reshape(depth, 1, n))


def _pre_kernel(*refs, first):
    if first:
        (x_ref, mod_ref, lng_ref, lnb_ref, w_in_ref, qg_ref, kvg_ref, w_uq_ref, w_uqr_ref, w_uk_ref, w_uv_ref,
         ones_ref, cos_ref, sin_ref, xn_ref, u_ref, ckv_ref, kpe_ref, q_ref, k_ref, v_ref) = refs
    else:
        (x_ref, mod_ref, w_in_ref, qg_ref, kvg_ref, w_uq_ref, w_uqr_ref, w_uk_ref, w_uv_ref,
         ones_ref, cos_ref, sin_ref, u_ref, ckv_ref, kpe_ref, q_ref, k_ref, v_ref) = refs
    x = x_ref[0]
    if first:
        x = _layer_norm(x, lng_ref[...], lnb_ref[...])
        xn_ref[0] = x
    sh1 = mod_ref[0, 0:1, :]
    sc1 = mod_ref[0, 1:2, :]
    h = (x * (1.0 + sc1) + sh1).astype(BF16)
    z = _dot(h, w_in_ref[...])
    o_cq = POOL_DIM
    o_kv = o_cq + Q_LORA_RANK
    o_pe = o_kv + KV_LORA_RANK
    u_ref[0] = z[:, :POOL_DIM]
    cqn = _rms_norm(z[:, o_cq:o_kv], qg_ref[...])
    ckv = _rms_norm(z[:, o_kv:o_pe], kvg_ref[...])
    ckv_ref[0] = ckv
    cos = cos_ref[...]
    sin = sin_ref[...]
    kpe = z[:, o_pe:o_pe + HEAD_LANES] * cos + z[:, o_pe + HEAD_LANES:] * sin
    kpe_ref[0] = kpe
    cqb = cqn.astype(BF16)
    ckvb = ckv.astype(BF16)
    qa = _dot(cqb, w_uq_ref[...])
    qb = _dot(cqb, w_uqr_ref[...])
    kn = _dot(ckvb, w_uk_ref[...])
    v_ref[0] = (_dot(ckvb, w_uv_ref[...]) + ones_ref[...]).astype(BF16)
    for hd in range(N_HEADS):
        sl = slice(hd * HEAD_LANES, (hd + 1) * HEAD_LANES)
        q_ref[0, :, sl] = ((qa[:, sl] * cos + qb[:, sl] * sin) * (SOFTMAX_SCALE * LOG2_E)).astype(BF16)
        k_ref[0, :, sl] = (kn[:, sl] + kpe).astype(BF16)


def _pre_call(x, mod, ln_in, w, cos, sin, *, first, tt):
    bsz, t, d = x.shape
    grid = (bsz, t // tt)
    tok = lambda n: pl.BlockSpec((1, tt, n), lambda b, i: (b, i, 0))
    full = lambda a: pl.BlockSpec(a.shape, lambda b, i: (0,) * a.ndim)
    in_arrays = [x, mod]
    in_specs = [tok(d), pl.BlockSpec((1, 8, d), lambda b, i: (b, 0, 0))]
    if first:
        in_arrays += list(ln_in)
        in_specs += [full(a) for a in ln_in]
    in_arrays += list(w)
    in_specs += [full(a) for a in w]
    in_arrays += [cos, sin]
    in_specs += [pl.BlockSpec((tt, HEAD_LANES), lambda b, i: (i, 0))] * 2
    out_shape, out_specs = [], []
    if first:
        out_shape.append(jax.ShapeDtypeStruct((bsz, t, d), F32))
        out_specs.append(tok(d))
    for n, dt in ((POOL_DIM, F32), (KV_LORA_RANK, F32), (HEAD_LANES, F32), (QK_LANES, BF16), (QK_LANES, BF16),
                  (VA_LANES, BF16)):
        out_shape.append(jax.ShapeDtypeStruct((bsz, t, n), dt))
        out_specs.append(tok(n))
    return pl.pallas_call(
        functools.partial(_pre_kernel, first=first),
        grid=grid, in_specs=in_specs, out_specs=out_specs, out_shape=out_shape,
        name="pre_mixer",
        compiler_params=_cparams(("arbitrary", "arbitrary")),
    )(*in_arrays)


def _pair_output(accs):
    outs = [accs[hh] / accs[hh][:, ONES_LANE[hh]:ONES_LANE[hh] + 1] for hh in range(2)]
    lane = lax.broadcasted_iota(jnp.int32, outs[0].shape, 1)
    return jnp.where(lane < V_HEAD_DIM, outs[0], outs[1])


def _prompt_attn_kernel(q_ref, k_ref, v_ref, o_ref, *, tq):
    qi = pl.program_id(2)
    heads = [slice(hh * HEAD_LANES, (hh + 1) * HEAD_LANES) for hh in range(2)]
    qs = [q_ref[0, :, hs] for hs in heads]

    def step(j, carry):
        start = pl.multiple_of(j * tq, tq)
        new = []
        for hh, hs in enumerate(heads):
            m, acc = carry[hh]
            s = _dot_nt(qs[hh], k_ref[0, pl.ds(start, tq), hs])
            m_new = jnp.maximum(m, jnp.max(s, -1, keepdims=True))
            p = jnp.exp2(s - m_new)
            acc = jnp.exp2(m - m_new) * acc + _dot(p.astype(BF16), v_ref[0, pl.ds(start, tq), hs])
            new.append((m_new, acc))
        return tuple(new)

    def diag_step(j, carry):
        start = pl.multiple_of(j * tq, tq)
        half = tq // 2
        new = []
        for hh, hs in enumerate(heads):
            m, acc = carry[hh]
            parts = []
            for r0, nk in ((0, half), (half, tq)):
                rows = slice(r0, r0 + half)
                row_chunk = (r0 + lax.broadcasted_iota(jnp.int32, (half, nk), 0)) // CHUNK
                col_chunk = lax.broadcasted_iota(jnp.int32, (half, nk), 1) // CHUNK
                s = _dot_nt(qs[hh][rows], k_ref[0, pl.ds(start, nk), hs])
                s = jnp.where(col_chunk <= row_chunk, s, -jnp.inf)
                m_new = jnp.maximum(m[rows], jnp.max(s, -1, keepdims=True))
                p = jnp.exp2(s - m_new)
                parts.append((m_new, jnp.exp2(m[rows] - m_new) * acc[rows]
                              + _dot(p.astype(BF16), v_ref[0, pl.ds(start, nk), hs])))
            new.append(tuple(jnp.concatenate([parts[0][c], parts[1][c]], 0) for c in range(2)))
        return tuple(new)

    init = tuple((jnp.full((tq, 1), -jnp.inf, F32), jnp.zeros((tq, HEAD_LANES), F32)) for _ in heads)

    def two_steps(i, carry):
        return step(2 * i + 1, step(2 * i, carry))

    carry = lax.fori_loop(0, qi // 2, two_steps, init)

    def finish(c):
        o_ref[0] = _pair_output([c[hh][1] for hh in range(2)]).astype(o_ref.dtype)

    @pl.when(qi % 2 == 0)
    def _():
        finish(diag_step(qi, carry))

    @pl.when(qi % 2 == 1)
    def _():
        finish(diag_step(qi, step(qi - 1, carry)))


def _prompt_attn_call(q, k, v, *, tq):
    bsz, t, _ = q.shape
    pair = 2 * HEAD_LANES
    resident = lambda: pl.BlockSpec((1, t, pair), lambda b, hp, i: (b, 0, hp))
    return pl.pallas_call(
        functools.partial(_prompt_attn_kernel, tq=tq),
        grid=(bsz, N_HEADS // 2, t // tq),
        in_specs=[pl.BlockSpec((1, tq, pair), lambda b, hp, i: (b, i, hp)), resident(), resident()],
        out_specs=pl.BlockSpec((1, tq, 2 * V_HEAD_DIM), lambda b, hp, i: (b, i, hp)),
        out_shape=jax.ShapeDtypeStruct((bsz, t, V_LANES), BF16),
        name="attn_prompt",
        compiler_params=_cparams(("arbitrary", "arbitrary", "arbitrary")),
    )(q, k, v)


def _sample_attn_kernel(q_ref, kn_ref, vn_ref, ckv_ref, cpe_ref, w_uk_ref, w_uv_ref, ones_ref, o_ref):
    ckvb = ckv_ref[0, 0].astype(BF16)
    cpe = cpe_ref[0, 0]
    for hp in range(N_HEADS // 2):
        accs = []
        for hh in range(2):
            hd = 2 * hp + hh
            hs = slice(hd * HEAD_LANES, (hd + 1) * HEAD_LANES)
            q = q_ref[0, :, hs]
            k_c = (_dot(ckvb, w_uk_ref[:, hs]) + cpe).astype(BF16)
            v_c = (_dot(ckvb, w_uv_ref[:, hs]) + ones_ref[:, hs]).astype(BF16)
            s_c = _dot_nt(q, k_c)
            s_n = _dot_nt(q, kn_ref[0, :, hs])
            m = jnp.maximum(jnp.max(s_c, -1, keepdims=True), jnp.max(s_n, -1, keepdims=True))
            p_c = jnp.exp2(s_c - m).astype(BF16)
            p_n = jnp.exp2(s_n - m).astype(BF16)
            accs.append(_dot(p_c, v_c) + _dot(p_n, vn_ref[0, :, hs]))
        vs = slice(hp * 2 * V_HEAD_DIM, (hp + 1) * 2 * V_HEAD_DIM)
        o_ref[0, :, vs] = _pair_output(accs).astype(o_ref.dtype)


def _sample_attn_call(q, kn, vn, cache_kv, cache_pe, w_uk, w_uv, ones_row, layer):
    bsz, t, _ = q.shape
    past = cache_kv.shape[2]
    tok = lambda n: pl.BlockSpec((1, t, n), lambda b: (b, 0, 0))
    full = lambda a: pl.BlockSpec(a.shape, lambda b: (0,) * a.ndim)
    return pl.pallas_call(
        _sample_attn_kernel,
        grid=(bsz,),
        in_specs=[tok(QK_LANES), tok(QK_LANES), tok(VA_LANES),
                  pl.BlockSpec((1, 1, past, KV_LORA_RANK), lambda b: (layer, b, 0, 0)),
                  pl.BlockSpec((1, 1, past, HEAD_LANES), lambda b: (layer, b, 0, 0)),
                  full(w_uk), full(w_uv), full(ones_row)],
        out_specs=tok(V_LANES),
        out_shape=jax.ShapeDtypeStruct((bsz, t, V_LANES), BF16),
        name="attn_sample",
        compiler_params=_cparams(("arbitrary",)),
    )(q, kn, vn, cache_kv, cache_pe, w_uk, w_uv, ones_row)


def _route_gates(logits_t, bias_t):
    score = [jax.nn.sigmoid(logits_t[e:e + 1, :]) for e in range(N_EXPERTS)]
    sel = [score[e] + bias_t[e:e + 1, :] for e in range(N_EXPERTS)]
    grp = []
    for g in range(N_EXPERT_GROUPS):
        mem = sel[g * EXPERTS_PER_GROUP:(g + 1) * EXPERTS_PER_GROUP]
        best = None
        for a in range(EXPERTS_PER_GROUP):
            for b in range(a + 1, EXPERTS_PER_GROUP):
                pair = mem[a] + mem[b]
                best = pair if best is None else jnp.maximum(best, pair)
        grp.append(best)
    in_best = []
    for g in range(N_EXPERT_GROUPS):
        ok = None
        for g2 in range(N_EXPERT_GROUPS):
            if g2 == g:
                continue
            c = (grp[g] > grp[g2]) if g2 < g else (grp[g] >= grp[g2])
            ok = c if ok is None else (ok & c)
        in_best.append(ok)
    picked = []
    for e in range(N_EXPERTS):
        g = e // EXPERTS_PER_GROUP
        rank = jnp.zeros_like(sel[e], dtype=jnp.int32)
        for e2 in range(g * EXPERTS_PER_GROUP, (g + 1) * EXPERTS_PER_GROUP):
            if e2 == e:
                continue
            ahead = (sel[e2] >= sel[e]) if e2 < e else (sel[e2] > sel[e])
            rank = rank + ahead.astype(jnp.int32)
        picked.append(in_best[g] & (rank < 2))
    wsel = [jnp.where(picked[e], score[e], 0.0) for e in range(N_EXPERTS)]
    total = wsel[0]
    for e in range(1, N_EXPERTS):
        total = total + wsel[e]
    return [wsel[e] / total for e in range(N_EXPERTS)]


def _post_kernel(x_ref, u_ref, halo_ref, left_ref, att_ref, mod_ref, w_pool_ref, pscale_ref, w_o_ref,
                 ln_g_ref, ln_b_ref, w_r_ref, rb_ref, x1_ref, hg_ref, ext_ref, *, tt, pos0, alpha):
    i = pl.program_id(1)
    u = u_ref[0]
    halo = jnp.where(i == 0, left_ref[0], halo_ref[0])
    ext_ref[0:HALO, :] = halo
    ext_ref[HALO:HALO + tt, :] = u
    sums = {}
    run = u
    for j in range(1, max(POOL_WINDOWS)):
        run = run + ext_ref[HALO - j:HALO - j + tt, :]
        if j + 1 in POOL_WINDOWS:
            sums[j + 1] = run
    lane = lax.broadcasted_iota(jnp.int32, (tt, POOL_DIM), 1)
    grp = lane // POOL_GROUP_DIM
    win_sum = sums[POOL_WINDOWS[-1]]
    win_len = jnp.full((tt, POOL_DIM), POOL_WINDOWS[-1], jnp.int32)
    for g in range(len(POOL_WINDOWS) - 2, -1, -1):
        win_sum = jnp.where(grp == g, sums[POOL_WINDOWS[g]], win_sum)
        win_len = jnp.where(grp == g, POOL_WINDOWS[g], win_len)
    pos = pos0 + i * tt + lax.broadcasted_iota(jnp.int32, (tt, POOL_DIM), 0)
    cnt = jnp.minimum(win_len, pos + 1).astype(F32)
    pooled = (win_sum / cnt - u).astype(BF16)
    y_pool = _dot(pooled, w_pool_ref[...]) * pscale_ref[...]
    y = _dot(y_pool.astype(BF16), w_o_ref[0:POOL_DIM, :]) + _dot(att_ref[0], w_o_ref[POOL_DIM:, :])
    g1 = mod_ref[0, 2:3, :]
    sh2 = mod_ref[0, 3:4, :]
    sc2 = mod_ref[0, 4:5, :]
    x1 = _layer_norm(alpha * x_ref[0] + (1.0 + g1) * y, ln_g_ref[...], ln_b_ref[...])
    x1_ref[0] = x1
    h2 = x1 * (1.0 + sc2) + sh2
    d = h2.shape[1]
    hg_ref[0, :, 0:d] = h2
    h_hi, h_lo = _split_bf16(h2)
    w_r = w_r_ref[...]
    l_hi = _dot_nt(w_r, h_hi)
    l_lo = _dot_nt(w_r[0:N_EXPERTS, :], h_lo)
    logits_t = l_hi[0:N_EXPERTS, :] + l_hi[N_EXPERTS:, :] + l_lo
    rows = _route_gates(logits_t, rb_ref[...])
    rows.append(jnp.zeros((GATE_LANES - N_EXPERTS, tt), F32))
    hg_ref[0, :, d:] = jnp.concatenate(rows, axis=0).T


def _post_call(x, u, left, att, mod, w, *, tt, pos0, alpha):
    bsz, t, d = x.shape
    per_tile = tt // HALO
    tok = lambda n: pl.BlockSpec((1, tt, n), lambda b, i: (b, i, 0))
    full = lambda a: pl.BlockSpec(a.shape, lambda b, i: (0,) * a.ndim)
    in_specs = [tok(d), tok(POOL_DIM),
                pl.BlockSpec((1, HALO, POOL_DIM), lambda b, i: (b, jnp.maximum(i * per_tile - 1, 0), 0)),
                pl.BlockSpec((1, HALO, POOL_DIM), lambda b, i: (b, 0, 0)),
                tok(V_LANES), pl.BlockSpec((1, 8, d), lambda b, i: (b, 0, 0))] + [full(a) for a in w]
    out_shape = [jax.ShapeDtypeStruct((bsz, t, d), F32), jax.ShapeDtypeStruct((bsz, t, d + GATE_LANES), F32)]
    out_specs = [tok(d), tok(d + GATE_LANES)]
    return pl.pallas_call(
        functools.partial(_post_kernel, tt=tt, pos0=pos0, alpha=alpha),
        grid=(bsz, t // tt), in_specs=in_specs, out_specs=out_specs, out_shape=out_shape,
        scratch_shapes=[pltpu.VMEM((HALO + tt, POOL_DIM), F32)],
        name="post_mixer",
        compiler_params=_cparams(("arbitrary", "arbitrary")),
    )(x, u, u, left, att, mod, *w)


def _expert_term(hg, w_gate, w_up, w_dn, expert):
    d = hg.shape[1] - GATE_LANES
    h = hg[:, :d].astype(BF16)
    gate_part = _dot(h, w_gate.astype(BF16))
    a = gate_part * jax.nn.sigmoid(gate_part) * _dot(h, w_up.astype(BF16))
    dn = _dot(a.astype(BF16), w_dn.astype(BF16))
    gates = hg[:, d:]
    lane = lax.broadcasted_iota(jnp.int32, gates.shape, 1)
    return jnp.sum(jnp.where(lane == expert, gates, 0.0), -1, keepdims=True) * dn


def _moe_kernel(x1_ref, hg_ref, g2_ref, w_gate_ref, w_up_ref, w_dn_ref, ln_g_ref, ln_b_ref, o_ref, acc_ref, *, alpha):
    e = pl.program_id(2)

    @pl.when(e == 0)
    def _():
        acc_ref[...] = jnp.zeros_like(acc_ref)

    acc_ref[...] += _expert_term(hg_ref[0], w_gate_ref[0], w_up_ref[0], w_dn_ref[0], e)

    @pl.when(e == N_EXPERTS - 1)
    def _():
        o_ref[0] = _layer_norm(alpha * x1_ref[0] + (1.0 + g2_ref[0]) * acc_ref[...], ln_g_ref[...], ln_b_ref[...])


def _moe_call(x1, hg, g2, w_gate, w_up, w_dn, ln_g, ln_b, layer, *, tt, alpha):
    bsz, t, d = x1.shape
    tok = lambda n: pl.BlockSpec((1, tt, n), lambda b, i, e: (b, i, 0))
    vec = pl.BlockSpec((1, d), lambda b, i, e: (0, 0))
    return pl.pallas_call(
        functools.partial(_moe_kernel, alpha=alpha),
        grid=(bsz, t // tt, N_EXPERTS),
        in_specs=[tok(d), tok(d + GATE_LANES), tok(d),
                  pl.BlockSpec((1, d, D_EXPERT), lambda b, i, e: (layer * N_EXPERTS + e, 0, 0)),
                  pl.BlockSpec((1, d, D_EXPERT), lambda b, i, e: (layer * N_EXPERTS + e, 0, 0)),
                  pl.BlockSpec((1, D_EXPERT, d), lambda b, i, e: (layer * N_EXPERTS + e, 0, 0)),
                  vec, vec],
        out_specs=tok(d),
        out_shape=jax.ShapeDtypeStruct((bsz, t, d), F32),
        scratch_shapes=[pltpu.VMEM((tt, d), F32)],
        name="moe_ln",
        compiler_params=_cparams(("arbitrary", "arbitrary", "arbitrary")),
    )(x1, hg, g2, w_gate, w_up, w_dn, ln_g, ln_b)


def _gather_rows(idx_ref, src_hbm, buf_ref, sem, rows):
    def row_copy(k):
        return pltpu.make_async_copy(src_hbm.at[pl.ds(idx_ref[k], 1)], buf_ref.at[pl.ds(k, 1)], sem)

    def issue(k, carry):
        row_copy(k).start()
        return carry

    def drain(k, carry):
        row_copy(k).wait()
        return carry

    lax.fori_loop(0, rows, issue, 0, unroll=8)
    lax.fori_loop(0, rows, drain, 0, unroll=16)


def _moe_grouped_kernel(tg_ref, tl_ref, idx_ref, hg_hbm, w_gate_ref, w_up_ref, w_dn_ref, y_ref, buf_ref, sem, *, tm):
    i = pl.program_id(0)
    e = pl.program_id(1)
    live = tl_ref[i] != 0

    @pl.when(jnp.logical_and(live, e == 0))
    def _():
        _gather_rows(idx_ref, hg_hbm, buf_ref, sem, tm)

    @pl.when(live)
    def _():
        term = _expert_term(buf_ref[...], w_gate_ref[0], w_up_ref[0], w_dn_ref[0], tg_ref[i] * EXPERTS_PER_GROUP + e)

        @pl.when(e == 0)
        def _():
            y_ref[...] = term

        @pl.when(e != 0)
        def _():
            y_ref[...] += term

    @pl.when(jnp.logical_and(jnp.logical_not(live), e == 0))
    def _():
        y_ref[...] = jnp.zeros_like(y_ref)


def _moe_grouped_call(hg, slot_token, tile_group, tile_live, w_gate, w_up, w_dn, layer, *, tm):
    n_slots = slot_token.shape[0]
    width = hg.shape[1]
    d = width - GATE_LANES
    expert = lambda i, e, tg, tl: (layer * N_EXPERTS + tg[i] * EXPERTS_PER_GROUP + e, 0, 0)
    grid_spec = pltpu.PrefetchScalarGridSpec(
        num_scalar_prefetch=2,
        grid=(n_slots // tm, EXPERTS_PER_GROUP),
        in_specs=[pl.BlockSpec((tm,), lambda i, e, tg, tl: (i,), memory_space=pltpu.SMEM),
                  pl.BlockSpec(memory_space=pl.ANY),
                  pl.BlockSpec((1, d, D_EXPERT), expert),
                  pl.BlockSpec((1, d, D_EXPERT), expert),
                  pl.BlockSpec((1, D_EXPERT, d), expert)],
        out_specs=pl.BlockSpec((tm, d), lambda i, e, tg, tl: (i, 0)),
        scratch_shapes=[pltpu.VMEM((tm, width), F32), pltpu.SemaphoreType.DMA(())],
    )
    return pl.pallas_call(
        functools.partial(_moe_grouped_kernel, tm=tm),
        grid_spec=grid_spec,
        out_shape=jax.ShapeDtypeStruct((n_slots, d), F32),
        name="moe_grouped",
        compiler_params=_cparams(("arbitrary", "arbitrary")),
    )(tile_group, tile_live, slot_token, hg, w_gate, w_up, w_dn)


def _ln_gather_kernel(idx_ref, x1_ref, ys_hbm, g2_ref, ln_g_ref, ln_b_ref, o_ref, buf_ref, sem, *, tt, alpha):
    _gather_rows(idx_ref, ys_hbm, buf_ref, sem, tt)
    o_ref[0] = _layer_norm(alpha * x1_ref[0] + (1.0 + g2_ref[0]) * buf_ref[...], ln_g_ref[...], ln_b_ref[...])


def _ln_gather_call(x1, ys, token_slot, g2, ln_g, ln_b, *, tt, alpha):
    bsz, t, d = x1.shape
    per_seq = t // tt
    tok = pl.BlockSpec((1, tt, d), lambda b, i: (b, i, 0))
    vec = pl.BlockSpec((1, d), lambda b, i: (0, 0))
    return pl.pallas_call(
        functools.partial(_ln_gather_kernel, tt=tt, alpha=alpha),
        grid=(bsz, per_seq),
        in_specs=[pl.BlockSpec((tt,), lambda b, i: (b * per_seq + i,), memory_space=pltpu.SMEM),
                  tok, pl.BlockSpec(memory_space=pl.ANY), pl.BlockSpec((1, 1, d), lambda b, i: (b, 0, 0)), vec, vec],
        out_specs=tok,
        out_shape=jax.ShapeDtypeStruct((bsz, t, d), F32),
        scratch_shapes=[pltpu.VMEM((tt, d), F32), pltpu.SemaphoreType.DMA(())],
        name="ln_gather",
        compiler_params=_cparams(("arbitrary", "arbitrary")),
    )(token_slot, x1, ys, g2, ln_g, ln_b)


def _dispatch_plan(gates, tm):
    n = gates.shape[0]
    n_slots = n + N_EXPERT_GROUPS * tm
    grp_mass = gates.reshape(n, N_EXPERT_GROUPS, EXPERTS_PER_GROUP).sum(-1)
    grp = jnp.argmax(grp_mass, -1).astype(jnp.int32)
    onehot = (grp[:, None] == jnp.arange(N_EXPERT_GROUPS, dtype=jnp.int32)[None, :]).astype(jnp.int32)
    counts = onehot.sum(0)
    padded = (counts + tm - 1) // tm * tm
    ends = jnp.cumsum(padded)
    starts = ends - padded
    rank = jnp.sum((jnp.cumsum(onehot, 0) - onehot) * onehot, -1)
    token_slot = (jnp.sum(starts[None, :] * onehot, -1) + rank).astype(jnp.int32)
    slot_token = jnp.zeros((n_slots,), jnp.int32).at[token_slot].set(jnp.arange(n, dtype=jnp.int32))
    tile_start = jnp.arange(n_slots // tm, dtype=jnp.int32) * tm
    tile_group = jnp.minimum(jnp.sum((tile_start[:, None] >= ends[None, :]).astype(jnp.int32), -1),
                             N_EXPERT_GROUPS - 1).astype(jnp.int32)
    tile_live = (tile_start < ends[-1]).astype(jnp.int32)
    return slot_token, token_slot, tile_group, tile_live


def _rope_tables(pos):
    half = QK_ROPE_DIM // 2
    inv_freq = ROPE_THETA ** (-jnp.arange(half, dtype=F32) / half)
    ang = pos.astype(F32)[:, None] * inv_freq[None, :]
    cos, sin = jnp.cos(ang), jnp.sin(ang)
    t = pos.shape[0]
    tail = HEAD_LANES - ROPE_OFF - QK_ROPE_DIM
    cos_t = jnp.concatenate([jnp.ones((t, ROPE_OFF), F32), cos, cos, jnp.zeros((t, tail), F32)], -1)
    sin_t = jnp.concatenate([jnp.zeros((t, ROPE_OFF), F32), sin, sin, jnp.zeros((t, tail), F32)], -1)
    return cos_t, sin_t


def _rot_partner(w):
    half = QK_ROPE_DIM // 2
    return jnp.concatenate([-w[..., half:], w[..., :half]], -1)


def _head_block(nope, rope):
    lead = nope.shape[:-1] if nope is not None else rope.shape[:-1]
    parts = [nope if nope is not None else jnp.zeros(lead + (QK_NOPE_DIM,), rope.dtype),
             rope if rope is not None else jnp.zeros(lead + (QK_ROPE_DIM,), nope.dtype)]
    parts.append(jnp.zeros(lead + (HEAD_LANES - QK_NOPE_DIM - QK_ROPE_DIM,), parts[0].dtype))
    out = jnp.concatenate(parts, -1)
    return out.reshape(out.shape[:-2] + (out.shape[-2] * HEAD_LANES,))


def _prep_weights(w_in, w_uq, w_uk, w_uv, w_pool, w_o, w_router, w_gate, w_up, w_down):
    depth = w_in.shape[0]
    o_pe = POOL_DIM + Q_LORA_RANK + KV_LORA_RANK
    w_pe = w_in[:, :, o_pe:]
    w_in_ext = jnp.concatenate([w_in[:, :, :o_pe], _head_block(None, w_pe[:, :, None, :]),
                                _head_block(None, _rot_partner(w_pe)[:, :, None, :])], -1).astype(BF16)
    uq = w_uq.reshape(depth, Q_LORA_RANK, N_HEADS, QK_NOPE_DIM + QK_ROPE_DIM)
    w_uq_pad = _head_block(uq[..., :QK_NOPE_DIM], uq[..., QK_NOPE_DIM:]).astype(BF16)
    w_uq_rot = _head_block(None, _rot_partner(uq[..., QK_NOPE_DIM:])).astype(BF16)
    w_uk_pad = _head_block(w_uk, None).astype(BF16)
    pad = jnp.zeros(w_uv.shape[:2] + (N_HEADS // 2, HEAD_LANES - V_HEAD_DIM), w_uv.dtype)
    uv = w_uv.reshape(depth, KV_LORA_RANK, N_HEADS // 2, 2, V_HEAD_DIM)
    w_uv_flat = jnp.concatenate([uv[:, :, :, 0], pad, pad, uv[:, :, :, 1]], -1).reshape(
        depth, KV_LORA_RANK, VA_LANES).astype(BF16)
    n_g = len(POOL_WINDOWS)
    eye = jnp.eye(n_g, dtype=F32)
    w_pool_bd = (w_pool[:, :, :, None, :] * eye[None, :, None, :, None]).reshape(depth, POOL_DIM, POOL_DIM).astype(BF16)
    r_hi, r_lo = _split_bf16(w_router.T)
    w_r = jnp.concatenate([r_hi, r_lo], 0)
    d = w_gate.shape[2]
    w_g = w_gate.reshape(depth * N_EXPERTS, d, D_EXPERT)
    w_u = w_up.reshape(depth * N_EXPERTS, d, D_EXPERT)
    w_dn = w_down.reshape(depth * N_EXPERTS, D_EXPERT, d)
    return w_in_ext, w_uq_pad, w_uq_rot, w_uk_pad, w_uv_flat, w_pool_bd, w_o.astype(BF16), w_r, w_g, w_u, w_dn


def _mod_blocks(mod_rows, d):
    depth, bsz, _ = mod_rows.shape
    m = mod_rows.reshape(depth, bsz, 6, d)
    return jnp.concatenate([m, jnp.zeros((depth, bsz, 2, d), F32)], 2)


def kernel(x_prompt, x_sample, cache_kv, cache_pe, state_pool, c_prompt, c_sample, ln_in_g, ln_in_b, w_ada, b_ada,
           w_in, q_norm_g, kv_norm_g, w_uq, w_uk, w_uv, w_pool, pool_scale, w_o, ln1_g, ln1_b, w_router,
           router_bias, w_gate, w_up, w_down, ln2_g, ln2_b):
    depth = w_in.shape[0]
    n_p, t_p, d = x_prompt.shape
    n_s, t_s, _ = x_sample.shape
    past = cache_kv.shape[2]
    alpha = float((2 * depth) ** 0.25)

    (w_in_ext, w_uq_pad, w_uq_rot, w_uk_pad, w_uv_flat, w_pool_bd, w_o_b, w_r, w_g, w_u, w_dn) = _prep_weights(
        w_in, w_uq, w_uk, w_uv, w_pool, w_o, w_router, w_gate, w_up, w_down)
    row = lambda a: a.reshape(1, -1)
    rb_t = router_bias.reshape(N_EXPERTS, 1)
    head_lane = jnp.arange(VA_LANES, dtype=jnp.int32) % HEAD_LANES
    odd_head = (jnp.arange(VA_LANES, dtype=jnp.int32) // HEAD_LANES) % 2 == 1
    ones_row = (head_lane == jnp.where(odd_head, ONES_LANE[1], ONES_LANE[0])).astype(F32).reshape(1, VA_LANES)

    n_c = n_p + n_s
    rows = -(-n_c // 8) * 8
    c_all = jnp.concatenate([c_prompt, c_sample, jnp.zeros((rows - n_c, d), F32)], 0)
    mod = _ada_call(c_all, w_ada, b_ada)
    mod_p = _mod_blocks(mod[:, :n_p], d)
    mod_s = _mod_blocks(mod[:, n_p:n_c], d)

    cos_p, sin_p = _rope_tables(jnp.arange(t_p, dtype=jnp.int32))
    cos_s, sin_s = _rope_tables(past + jnp.arange(t_s, dtype=jnp.int32))
    cache_pe_blk = _head_block(None, cache_pe[:, :, :, None, :])
    left_p = jnp.zeros((depth, n_p, HALO, POOL_DIM), F32)
    left_s = jnp.concatenate([jnp.zeros((depth, n_s, HALO - POOL_CTX, POOL_DIM), F32), state_pool], 2)

    tt_p = min(512, t_p)
    tt_s = t_s

    def run(x, mod_g, cos, sin, left, tt, pos0, prompt):
        kvs, pes, pools = [], [], []
        for l in range(depth):
            w_pre = (w_in_ext[l], row(q_norm_g[l]), row(kv_norm_g[l]), w_uq_pad[l], w_uq_rot[l], w_uk_pad[l],
                     w_uv_flat[l], ones_row)
            outs = _pre_call(x, mod_g[l], (row(ln_in_g), row(ln_in_b)), w_pre, cos, sin, first=(l == 0), tt=tt)
            if l == 0:
                x = outs[0]
                outs = outs[1:]
            u, ckv, kpe, q, k, v = outs
            if prompt:
                att = _prompt_attn_call(q, k, v, tq=min(ATTN_TILE, x.shape[1]))
            else:
                att = _sample_attn_call(q, k, v, cache_kv, cache_pe_blk, w_uk_pad[l], w_uv_flat[l], ones_row, l)
            w_post = (w_pool_bd[l], row(pool_scale[l]), w_o_b[l], row(ln1_g[l]), row(ln1_b[l]), w_r, rb_t)
            x1, hg = _post_call(x, u, left[l], att, mod_g[l], w_post, tt=tt, pos0=pos0, alpha=alpha)
            g2 = mod_g[l][:, 5:6, :]
            bsz, t, _ = x1.shape
            if prompt:
                hg_rows = hg.reshape(bsz * t, d + GATE_LANES)
                slot_token, token_slot, tile_group, tile_live = _dispatch_plan(hg_rows[:, d:d + N_EXPERTS], MOE_TILE)
                ys = _moe_grouped_call(hg_rows, slot_token, tile_group, tile_live, w_g, w_u, w_dn, l, tm=MOE_TILE)
                x = _ln_gather_call(x1, ys, token_slot, g2, row(ln2_g[l]), row(ln2_b[l]), tt=tt, alpha=alpha)
            else:
                flat = lambda a: a.reshape(1, bsz * t, a.shape[-1])
                g2_tok = jnp.broadcast_to(g2, (bsz, t, d))
                x = _moe_call(flat(x1), flat(hg), flat(g2_tok), w_g, w_u, w_dn, row(ln2_g[l]),
                              row(ln2_b[l]), l, tt=bsz * t, alpha=alpha).reshape(bsz, t, d)
            kvs.append(ckv)
            pes.append(kpe[:, :, ROPE_OFF:ROPE_OFF + QK_ROPE_DIM])
            pools.append(u[:, -POOL_CTX:, :])
        return x, jnp.stack(kvs), jnp.stack(pes), jnp.stack(pools)

    y_p, kv_p, pe_p, pool_p = run(x_prompt, mod_p, cos_p, sin_p, left_p, tt_p, 0, True)
    y_s, kv_s, pe_s, pool_s = run(x_sample, mod_s, cos_s, sin_s, left_s, tt_s, past, False)
    return (y_p, y_s, kv_p, pe_p, pool_p, kv_s, pe_s, pool_s)
```

```python
import functools

import jax
import jax.numpy as jnp
from jax import lax
from jax.experimental import pallas as pl
from jax.experimental.pallas import tpu as pltpu

F32 = jnp.float32
BF16 = jnp.bfloat16

CHUNK = 64
POOL_WINDOWS = (2, 4, 8, 16)
POOL_DIM = 256
POOL_GROUP_DIM = 64
POOL_CTX = 15
HALO = 16
QK_NOPE_DIM = 64
QK_ROPE_DIM = 32
V_HEAD_DIM = 64
N_HEADS = 12
Q_LORA_RANK = 384
KV_LORA_RANK = 256
ROPE_THETA = 10000.0
SOFTMAX_SCALE = (QK_NOPE_DIM + QK_ROPE_DIM) ** -0.5
N_EXPERTS = 16
N_EXPERT_GROUPS = 4
EXPERTS_PER_GROUP = 4
D_EXPERT = 512
LN_EPS = 1e-5
RMS_EPS = 1e-6

HEAD_LANES = 128
ROPE_OFF = QK_NOPE_DIM
QK_LANES = N_HEADS * HEAD_LANES
V_LANES = N_HEADS * V_HEAD_DIM
VA_LANES = N_HEADS * HEAD_LANES
ONES_LANE = (V_HEAD_DIM, 0)
LOG2_E = 1.4426950408889634
Z_LANES = POOL_DIM + Q_LORA_RANK + KV_LORA_RANK + 2 * HEAD_LANES
GATE_LANES = 128

VMEM_LIMIT = 56 * 1024 * 1024
ATTN_TILE = 1024
MOE_TILE = 1024


def _cparams(sem):
    return pltpu.CompilerParams(dimension_semantics=sem, vmem_limit_bytes=VMEM_LIMIT)


def _layer_norm(x, g, b):
    mu = jnp.mean(x, -1, keepdims=True)
    xc = x - mu
    var = jnp.mean(xc * xc, -1, keepdims=True)
    return xc * lax.rsqrt(var + LN_EPS) * g + b


def _rms_norm(x, g):
    return x * lax.rsqrt(jnp.mean(x * x, -1, keepdims=True) + RMS_EPS) * g


def _dot(a, b):
    return jnp.dot(a, b, preferred_element_type=F32)


def _dot_nt(a, b):
    return lax.dot_general(a, b, (((1,), (1,)), ((), ())), preferred_element_type=F32)


def _split_bf16(x):
    hi = x.astype(BF16)
    lo = (x - hi.astype(F32)).astype(BF16)
    return hi, lo


def _ada_kernel(c_ref, w_ref, b_ref, o_ref):
    c = c_ref[...]
    s = c * jax.nn.sigmoid(c)
    s_hi, s_lo = _split_bf16(s)
    w_hi, w_lo = _split_bf16(w_ref[0])
    o_ref[0] = _dot(s_hi, w_hi) + _dot(s_lo, w_hi) + _dot(s_hi, w_lo) + b_ref[0]


def _ada_call(c_all, w_ada, b_ada):
    depth, d, n = w_ada.shape
    rows = c_all.shape[0]
    tn = 1536
    return pl.pallas_call(
        _ada_kernel,
        grid=(depth, n // tn),
        in_specs=[pl.BlockSpec((rows, d), lambda l, j: (0, 0)),
                  pl.BlockSpec((1, d, tn), lambda l, j: (l, 0, j)),
                  pl.BlockSpec((1, 1, tn), lambda l, j: (l, 0, j))],
        out_specs=pl.BlockSpec((1, rows, tn), lambda l, j: (l, 0, j)),
        out_shape=jax.ShapeDtypeStruct((depth, rows, n), F32),
        name="ada_mod",
        compiler_params=_cparams(("arbitrary", "arbitrary")),
    )(c_all, w_ada, b_ada.reshape(depth, 1, n))


def _pre_kernel(*refs, first):
    if first:
        (x_ref, mod_ref, lng_ref, lnb_ref, w_in_ref, qg_ref, kvg_ref, w_uq_ref, w_uqr_ref, w_uk_ref, w_uv_ref,
         ones_ref, cos_ref, sin_ref, xn_ref, u_ref, ckv_ref, kpe_ref, q_ref, k_ref, v_ref) = refs
    else:
        (x_ref, mod_ref, w_in_ref, qg_ref, kvg_ref, w_uq_ref, w_uqr_ref, w_uk_ref, w_uv_ref,
         ones_ref, cos_ref, sin_ref, u_ref, ckv_ref, kpe_ref, q_ref, k_ref, v_ref) = refs
    x = x_ref[0]
    if first:
        x = _layer_norm(x, lng_ref[...], lnb_ref[...])
        xn_ref[0] = x
    sh1 = mod_ref[0, 0:1, :]
    sc1 = mod_ref[0, 1:2, :]
    h = (x * (1.0 + sc1) + sh1).astype(BF16)
    z = _dot(h, w_in_ref[...])
    o_cq = POOL_DIM
    o_kv = o_cq + Q_LORA_RANK
    o_pe = o_kv + KV_LORA_RANK
    u_ref[0] = z[:, :POOL_DIM]
    cqn = _rms_norm(z[:, o_cq:o_kv], qg_ref[...])
    ckv = _rms_norm(z[:, o_kv:o_pe], kvg_ref[...])
    ckv_ref[0] = ckv
    cos = cos_ref[...]
    sin = sin_ref[...]
    kpe = z[:, o_pe:o_pe + HEAD_LANES] * cos + z[:, o_pe + HEAD_LANES:] * sin
    kpe_ref[0] = kpe
    cqb = cqn.astype(BF16)
    ckvb = ckv.astype(BF16)
    qa = _dot(cqb, w_uq_ref[...])
    qb = _dot(cqb, w_uqr_ref[...])
    kn = _dot(ckvb, w_uk_ref[...])
    v_ref[0] = (_dot(ckvb, w_uv_ref[...]) + ones_ref[...]).astype(BF16)
    for hd in range(N_HEADS):
        sl = slice(hd * HEAD_LANES, (hd + 1) * HEAD_LANES)
        q_ref[0, :, sl] = ((qa[:, sl] * cos + qb[:, sl] * sin) * (SOFTMAX_SCALE * LOG2_E)).astype(BF16)
        k_ref[0, :, sl] = (kn[:, sl] + kpe).astype(BF16)


def _pre_call(x, mod, ln_in, w, cos, sin, *, first, tt):
    bsz, t, d = x.shape
    grid = (bsz, t // tt)
    tok = lambda n: pl.BlockSpec((1, tt, n), lambda b, i: (b, i, 0))
    full = lambda a: pl.BlockSpec(a.shape, lambda b, i: (0,) * a.ndim)
    in_arrays = [x, mod]
    in_specs = [tok(d), pl.BlockSpec((1, 8, d), lambda b, i: (b, 0, 0))]
    if first:
        in_arrays += list(ln_in)
        in_specs += [full(a) for a in ln_in]
    in_arrays += list(w)
    in_specs += [full(a) for a in w]
    in_arrays += [cos, sin]
    in_specs += [pl.BlockSpec((tt, HEAD_LANES), lambda b, i: (i, 0))] * 2
    out_shape, out_specs = [], []
    if first:
        out_shape.append(jax.ShapeDtypeStruct((bsz, t, d), F32))
        out_specs.append(tok(d))
    for n, dt in ((POOL_DIM, F32), (KV_LORA_RANK, F32), (HEAD_LANES, F32), (QK_LANES, BF16), (QK_LANES, BF16),
                  (VA_LANES, BF16)):
        out_shape.append(jax.ShapeDtypeStruct((bsz, t, n), dt))
        out_specs.append(tok(n))
    return pl.pallas_call(
        functools.partial(_pre_kernel, first=first),
        grid=grid, in_specs=in_specs, out_specs=out_specs, out_shape=out_shape,
        name="pre_mixer",
        compiler_params=_cparams(("arbitrary", "arbitrary")),
    )(*in_arrays)


def _pair_output(accs):
    outs = [accs[hh] / accs[hh][:, ONES_LANE[hh]:ONES_LANE[hh] + 1] for hh in range(2)]
    lane = lax.broadcasted_iota(jnp.int32, outs[0].shape, 1)
    return jnp.where(lane < V_HEAD_DIM, outs[0], outs[1])


def _prompt_attn_kernel(q_ref, k_ref, v_ref, o_ref, *, tq):
    qi = pl.program_id(2)
    heads = [slice(hh * HEAD_LANES, (hh + 1) * HEAD_LANES) for hh in range(2)]
    qs = [q_ref[0, :, hs] for hs in heads]

    def step(j, carry):
        start = pl.multiple_of(j * tq, tq)
        new = []
        for hh, hs in enumerate(heads):
            m, acc = carry[hh]
            s = _dot_nt(qs[hh], k_ref[0, pl.ds(start, tq), hs])
            m_new = jnp.maximum(m, jnp.max(s, -1, keepdims=True))
            p = jnp.exp2(s - m_new)
            acc = jnp.exp2(m - m_new) * acc + _dot(p.astype(BF16), v_ref[0, pl.ds(start, tq), hs])
            new.append((m_new, acc))
        return tuple(new)

    def diag_step(j, carry):
        start = pl.multiple_of(j * tq, tq)
        half = tq // 2
        new = []
        for hh, hs in enumerate(heads):
            m, acc = carry[hh]
            parts = []
            for r0, nk in ((0, half), (half, tq)):
                rows = slice(r0, r0 + half)
                row_chunk = (r0 + lax.broadcasted_iota(jnp.int32, (half, nk), 0)) // CHUNK
                col_chunk = lax.broadcasted_iota(jnp.int32, (half, nk), 1) // CHUNK
                s = _dot_nt(qs[hh][rows], k_ref[0, pl.ds(start, nk), hs])
                s = jnp.where(col_chunk <= row_chunk, s, -jnp.inf)
                m_new = jnp.maximum(m[rows], jnp.max(s, -1, keepdims=True))
                p = jnp.exp2(s - m_new)
                parts.append((m_new, jnp.exp2(m[rows] - m_new) * acc[rows]
                              + _dot(p.astype(BF16), v_ref[0, pl.ds(start, nk), hs])))
            new.append(tuple(jnp.concatenate([parts[0][c], parts[1][c]], 0) for c in range(2)))
        return tuple(new)

    init = tuple((jnp.full((tq, 1), -jnp.inf, F32), jnp.zeros((tq, HEAD_LANES), F32)) for _ in heads)

    def two_steps(i, carry):
        return step(2 * i + 1, step(2 * i, carry))

    carry = lax.fori_loop(0, qi // 2, two_steps, init)

    def finish(c):
        o_ref[0] = _pair_output([c[hh][1] for hh in range(2)]).astype(o_ref.dtype)

    @pl.when(qi % 2 == 0)
    def _():
        finish(diag_step(qi, carry))

    @pl.when(qi % 2 == 1)
    def _():
        finish(diag_step(qi, step(qi - 1, carry)))


def _prompt_attn_call(q, k, v, *, tq):
    bsz, t, _ = q.shape
    pair = 2 * HEAD_LANES
    resident = lambda: pl.BlockSpec((1, t, pair), lambda b, hp, i: (b, 0, hp))
    return pl.pallas_call(
        functools.partial(_prompt_attn_kernel, tq=tq),
        grid=(bsz, N_HEADS // 2, t // tq),
        in_specs=[pl.BlockSpec((1, tq, pair), lambda b, hp, i: (b, i, hp)), resident(), resident()],
        out_specs=pl.BlockSpec((1, tq, 2 * V_HEAD_DIM), lambda b, hp, i: (b, i, hp)),
        out_shape=jax.ShapeDtypeStruct((bsz, t, V_LANES), BF16),
        name="attn_prompt",
        compiler_params=_cparams(("arbitrary", "arbitrary", "arbitrary")),
    )(q, k, v)


def _sample_attn_kernel(q_ref, kn_ref, vn_ref, ckv_ref, cpe_ref, w_uk_ref, w_uv_ref, ones_ref, o_ref):
    ckvb = ckv_ref[0, 0].astype(BF16)
    cpe = cpe_ref[0, 0]
    for hp in range(N_HEADS // 2):
        accs = []
        for hh in range(2):
            hd = 2 * hp + hh
            hs = slice(hd * HEAD_LANES, (hd + 1) * HEAD_LANES)
            q = q_ref[0, :, hs]
            k_c = (_dot(ckvb, w_uk_ref[:, hs]) + cpe).astype(BF16)
            v_c = (_dot(ckvb, w_uv_ref[:, hs]) + ones_ref[:, hs]).astype(BF16)
            s_c = _dot_nt(q, k_c)
            s_n = _dot_nt(q, kn_ref[0, :, hs])
            m = jnp.maximum(jnp.max(s_c, -1, keepdims=True), jnp.max(s_n, -1, keepdims=True))
            p_c = jnp.exp2(s_c - m).astype(BF16)
            p_n = jnp.exp2(s_n - m).astype(BF16)
            accs.append(_dot(p_c, v_c) + _dot(p_n, vn_ref[0, :, hs]))
        vs = slice(hp * 2 * V_HEAD_DIM, (hp + 1) * 2 * V_HEAD_DIM)
        o_ref[0, :, vs] = _pair_output(accs).astype(o_ref.dtype)


def _sample_attn_call(q, kn, vn, cache_kv, cache_pe, w_uk, w_uv, ones_row, layer):
    bsz, t, _ = q.shape
    past = cache_kv.shape[2]
    tok = lambda n: pl.BlockSpec((1, t, n), lambda b: (b, 0, 0))
    full = lambda a: pl.BlockSpec(a.shape, lambda b: (0,) * a.ndim)
    return pl.pallas_call(
        _sample_attn_kernel,
        grid=(bsz,),
        in_specs=[tok(QK_LANES), tok(QK_LANES), tok(VA_LANES),
                  pl.BlockSpec((1, 1, past, KV_LORA_RANK), lambda b: (layer, b, 0, 0)),
                  pl.BlockSpec((1, 1, past, HEAD_LANES), lambda b: (layer, b, 0, 0)),
                  full(w_uk), full(w_uv), full(ones_row)],
        out_specs=tok(V_LANES),
        out_shape=jax.ShapeDtypeStruct((bsz, t, V_LANES), BF16),
        name="attn_sample",
        compiler_params=_cparams(("arbitrary",)),
    )(q, kn, vn, cache_kv, cache_pe, w_uk, w_uv, ones_row)


def _route_gates(logits_t, bias_t):
    score = [jax.nn.sigmoid(logits_t[e:e + 1, :]) for e in range(N_EXPERTS)]
    sel = [score[e] + bias_t[e:e + 1, :] for e in range(N_EXPERTS)]
    grp = []
    for g in range(N_EXPERT_GROUPS):
        mem = sel[g * EXPERTS_PER_GROUP:(g + 1) * EXPERTS_PER_GROUP]
        best = None
        for a in range(EXPERTS_PER_GROUP):
            for b in range(a + 1, EXPERTS_PER_GROUP):
                pair = mem[a] + mem[b]
                best = pair if best is None else jnp.maximum(best, pair)
        grp.append(best)
    in_best = []
    for g in range(N_EXPERT_GROUPS):
        ok = None
        for g2 in range(N_EXPERT_GROUPS):
            if g2 == g:
                continue
            c = (grp[g] > grp[g2]) if g2 < g else (grp[g] >= grp[g2])
            ok = c if ok is None else (ok & c)
        in_best.append(ok)
    picked = []
    for e in range(N_EXPERTS):
        g = e // EXPERTS_PER_GROUP
        rank = jnp.zeros_like(sel[e], dtype=jnp.int32)
        for e2 in range(g * EXPERTS_PER_GROUP, (g + 1) * EXPERTS_PER_GROUP):
            if e2 == e:
                continue
            ahead = (sel[e2] >= sel[e]) if e2 < e else (sel[e2] > sel[e])
            rank = rank + ahead.astype(jnp.int32)
        picked.append(in_best[g] & (rank < 2))
    wsel = [jnp.where(picked[e], score[e], 0.0) for e in range(N_EXPERTS)]
    total = wsel[0]
    for e in range(1, N_EXPERTS):
        total = total + wsel[e]
    return [wsel[e] / total for e in range(N_EXPERTS)]


def _post_kernel(x_ref, u_ref, halo_ref, left_ref, att_ref, mod_ref, w_pool_ref, pscale_ref, w_o_ref,
                 ln_g_ref, ln_b_ref, w_r_ref, rb_ref, x1_ref, hg_ref, ext_ref, *, tt, pos0, alpha):
    i = pl.program_id(1)
    u = u_ref[0]
    halo = jnp.where(i == 0, left_ref[0], halo_ref[0])
    ext_ref[0:HALO, :] = halo
    ext_ref[HALO:HALO + tt, :] = u
    sums = {}
    run = u
    for j in range(1, max(POOL_WINDOWS)):
        run = run + ext_ref[HALO - j:HALO - j + tt, :]
        if j + 1 in POOL_WINDOWS:
            sums[j + 1] = run
    lane = lax.broadcasted_iota(jnp.int32, (tt, POOL_DIM), 1)
    grp = lane // POOL_GROUP_DIM
    win_sum = sums[POOL_WINDOWS[-1]]
    win_len = jnp.full((tt, POOL_DIM), POOL_WINDOWS[-1], jnp.int32)
    for g in range(len(POOL_WINDOWS) - 2, -1, -1):
        win_sum = jnp.where(grp == g, sums[POOL_WINDOWS[g]], win_sum)
        win_len = jnp.where(grp == g, POOL_WINDOWS[g], win_len)
    pos = pos0 + i * tt + lax.broadcasted_iota(jnp.int32, (tt, POOL_DIM), 0)
    cnt = jnp.minimum(win_len, pos + 1).astype(F32)
    pooled = (win_sum / cnt - u).astype(BF16)
    y_pool = _dot(pooled, w_pool_ref[...]) * pscale_ref[...]
    y = _dot(y_pool.astype(BF16), w_o_ref[0:POOL_DIM, :]) + _dot(att_ref[0], w_o_ref[POOL_DIM:, :])
    g1 = mod_ref[0, 2:3, :]
    sh2 = mod_ref[0, 3:4, :]
    sc2 = mod_ref[0, 4:5, :]
    x1 = _layer_norm(alpha * x_ref[0] + (1.0 + g1) * y, ln_g_ref[...], ln_b_ref[...])
    x1_ref[0] = x1
    h2 = x1 * (1.0 + sc2) + sh2
    d = h2.shape[1]
    hg_ref[0, :, 0:d] = h2
    h_hi, h_lo = _split_bf16(h2)
    w_r = w_r_ref[...]
    l_hi = _dot_nt(w_r, h_hi)
    l_lo = _dot_nt(w_r[0:N_EXPERTS, :], h_lo)
    logits_t = l_hi[0:N_EXPERTS, :] + l_hi[N_EXPERTS:, :] + l_lo
    rows = _route_gates(logits_t, rb_ref[...])
    rows.append(jnp.zeros((GATE_LANES - N_EXPERTS, tt), F32))
    hg_ref[0, :, d:] = jnp.concatenate(rows, axis=0).T


def _post_call(x, u, left, att, mod, w, *, tt, pos0, alpha):
    bsz, t, d = x.shape
    per_tile = tt // HALO
    tok = lambda n: pl.BlockSpec((1, tt, n), lambda b, i: (b, i, 0))
    full = lambda a: pl.BlockSpec(a.shape, lambda b, i: (0,) * a.ndim)
    in_specs = [tok(d), tok(POOL_DIM),
                pl.BlockSpec((1, HALO, POOL_DIM), lambda b, i: (b, jnp.maximum(i * per_tile - 1, 0), 0)),
                pl.BlockSpec((1, HALO, POOL_DIM), lambda b, i: (b, 0, 0)),
                tok(V_LANES), pl.BlockSpec((1, 8, d), lambda b, i: (b, 0, 0))] + [full(a) for a in w]
    out_shape = [jax.ShapeDtypeStruct((bsz, t, d), F32), jax.ShapeDtypeStruct((bsz, t, d + GATE_LANES), F32)]
    out_specs = [tok(d), tok(d + GATE_LANES)]
    return pl.pallas_call(
        functools.partial(_post_kernel, tt=tt, pos0=pos0, alpha=alpha),
        grid=(bsz, t // tt), in_specs=in_specs, out_specs=out_specs, out_shape=out_shape,
        scratch_shapes=[pltpu.VMEM((HALO + tt, POOL_DIM), F32)],
        name="post_mixer",
        compiler_params=_cparams(("arbitrary", "arbitrary")),
    )(x, u, u, left, att, mod, *w)


def _expert_term(hg, w_gate, w_up, w_dn, expert):
    d = hg.shape[1] - GATE_LANES
    h = hg[:, :d].astype(BF16)
    gate_part = _dot(h, w_gate.astype(BF16))
    a = gate_part * jax.nn.sigmoid(gate_part) * _dot(h, w_up.astype(BF16))
    dn = _dot(a.astype(BF16), w_dn.astype(BF16))
    gates = hg[:, d:]
    lane = lax.broadcasted_iota(jnp.int32, gates.shape, 1)
    return jnp.sum(jnp.where(lane == expert, gates, 0.0), -1, keepdims=True) * dn


def _moe_kernel(x1_ref, hg_ref, g2_ref, w_gate_ref, w_up_ref, w_dn_ref, ln_g_ref, ln_b_ref, o_ref, acc_ref, *, alpha):
    e = pl.program_id(2)

    @pl.when(e == 0)
    def _():
        acc_ref[...] = jnp.zeros_like(acc_ref)

    acc_ref[...] += _expert_term(hg_ref[0], w_gate_ref[0], w_up_ref[0], w_dn_ref[0], e)

    @pl.when(e == N_EXPERTS - 1)
    def _():
        o_ref[0] = _layer_norm(alpha * x1_ref[0] + (1.0 + g2_ref[0]) * acc_ref[...], ln_g_ref[...], ln_b_ref[...])


def _moe_call(x1, hg, g2, w_gate, w_up, w_dn, ln_g, ln_b, layer, *, tt, alpha):
    bsz, t, d = x1.shape
    tok = lambda n: pl.BlockSpec((1, tt, n), lambda b, i, e: (b, i, 0))
    vec = pl.BlockSpec((1, d), lambda b, i, e: (0, 0))
    return pl.pallas_call(
        functools.partial(_moe_kernel, alpha=alpha),
        grid=(bsz, t // tt, N_EXPERTS),
        in_specs=[tok(d), tok(d + GATE_LANES), tok(d),
                  pl.BlockSpec((1, d, D_EXPERT), lambda b, i, e: (layer * N_EXPERTS + e, 0, 0)),
                  pl.BlockSpec((1, d, D_EXPERT), lambda b, i, e: (layer * N_EXPERTS + e, 0, 0)),
                  pl.BlockSpec((1, D_EXPERT, d), lambda b, i, e: (layer * N_EXPERTS + e, 0, 0)),
                  vec, vec],
        out_specs=tok(d),
        out_shape=jax.ShapeDtypeStruct((bsz, t, d), F32),
        scratch_shapes=[pltpu.VMEM((tt, d), F32)],
        name="moe_ln",
        compiler_params=_cparams(("arbitrary", "arbitrary", "arbitrary")),
    )(x1, hg, g2, w_gate, w_up, w_dn, ln_g, ln_b)


def _gather_rows(idx_ref, src_hbm, buf_ref, sem, rows):
    def row_copy(k):
        return pltpu.make_async_copy(src_hbm.at[pl.ds(idx_ref[k], 1)], buf_ref.at[pl.ds(k, 1)], sem)

    def issue(k, carry):
        row_copy(k).start()
        return carry

    def drain(k, carry):
        row_copy(k).wait()
        return carry

    lax.fori_loop(0, rows, issue, 0, unroll=8)
    lax.fori_loop(0, rows, drain, 0, unroll=16)


def _moe_grouped_kernel(tg_ref, tl_ref, idx_ref, hg_hbm, w_gate_ref, w_up_ref, w_dn_ref, y_ref, buf_ref, sem, *, tm):
    i = pl.program_id(0)
    e = pl.program_id(1)
    live = tl_ref[i] != 0

    @pl.when(jnp.logical_and(live, e == 0))
    def _():
        _gather_rows(idx_ref, hg_hbm, buf_ref, sem, tm)

    @pl.when(live)
    def _():
        term = _expert_term(buf_ref[...], w_gate_ref[0], w_up_ref[0], w_dn_ref[0], tg_ref[i] * EXPERTS_PER_GROUP + e)

        @pl.when(e == 0)
        def _():
            y_ref[...] = term

        @pl.when(e != 0)
        def _():
            y_ref[...] += term

    @pl.when(jnp.logical_and(jnp.logical_not(live), e == 0))
    def _():
        y_ref[...] = jnp.zeros_like(y_ref)


def _moe_grouped_call(hg, slot_token, tile_group, tile_live, w_gate, w_up, w_dn, layer, *, tm):
    n_slots = slot_token.shape[0]
    width = hg.shape[1]
    d = width - GATE_LANES
    expert = lambda i, e, tg, tl: (layer * N_EXPERTS + tg[i] * EXPERTS_PER_GROUP + e, 0, 0)
    grid_spec = pltpu.PrefetchScalarGridSpec(
        num_scalar_prefetch=2,
        grid=(n_slots // tm, EXPERTS_PER_GROUP),
        in_specs=[pl.BlockSpec((tm,), lambda i, e, tg, tl: (i,), memory_space=pltpu.SMEM),
                  pl.BlockSpec(memory_space=pl.ANY),
                  pl.BlockSpec((1, d, D_EXPERT), expert),
                  pl.BlockSpec((1, d, D_EXPERT), expert),
                  pl.BlockSpec((1, D_EXPERT, d), expert)],
        out_specs=pl.BlockSpec((tm, d), lambda i, e, tg, tl: (i, 0)),
        scratch_shapes=[pltpu.VMEM((tm, width), F32), pltpu.SemaphoreType.DMA(())],
    )
    return pl.pallas_call(
        functools.partial(_moe_grouped_kernel, tm=tm),
        grid_spec=grid_spec,
        out_shape=jax.ShapeDtypeStruct((n_slots, d), F32),
        name="moe_grouped",
        compiler_params=_cparams(("arbitrary", "arbitrary")),
    )(tile_group, tile_live, slot_token, hg, w_gate, w_up, w_dn)


def _ln_gather_kernel(idx_ref, x1_ref, ys_hbm, g2_ref, ln_g_ref, ln_b_ref, o_ref, buf_ref, sem, *, tt, alpha):
    _gather_rows(idx_ref, ys_hbm, buf_ref, sem, tt)
    o_ref[0] = _layer_norm(alpha * x1_ref[0] + (1.0 + g2_ref[0]) * buf_ref[...], ln_g_ref[...], ln_b_ref[...])


def _ln_gather_call(x1, ys, token_slot, g2, ln_g, ln_b, *, tt, alpha):
    bsz, t, d = x1.shape
    per_seq = t // tt
    tok = pl.BlockSpec((1, tt, d), lambda b, i: (b, i, 0))
    vec = pl.BlockSpec((1, d), lambda b, i: (0, 0))
    return pl.pallas_call(
        functools.partial(_ln_gather_kernel, tt=tt, alpha=alpha),
        grid=(bsz, per_seq),
        in_specs=[pl.BlockSpec((tt,), lambda b, i: (b * per_seq + i,), memory_space=pltpu.SMEM),
                  tok, pl.BlockSpec(memory_space=pl.ANY), pl.BlockSpec((1, 1, d), lambda b, i: (b, 0, 0)), vec, vec],
        out_specs=tok,
        out_shape=jax.ShapeDtypeStruct((bsz, t, d), F32),
        scratch_shapes=[pltpu.VMEM((tt, d), F32), pltpu.SemaphoreType.DMA(())],
        name="ln_gather",
        compiler_params=_cparams(("arbitrary", "arbitrary")),
    )(token_slot, x1, ys, g2, ln_g, ln_b)


def _dispatch_plan(gates, tm):
    n = gates.shape[0]
    n_slots = n + N_EXPERT_GROUPS * tm
    grp_mass = gates.reshape(n, N_EXPERT_GROUPS, EXPERTS_PER_GROUP).sum(-1)
    grp = jnp.argmax(grp_mass, -1).astype(jnp.int32)
    onehot = (grp[:, None] == jnp.arange(N_EXPERT_GROUPS, dtype=jnp.int32)[None, :]).astype(jnp.int32)
    counts = onehot.sum(0)
    padded = (counts + tm - 1) // tm * tm
    ends = jnp.cumsum(padded)
    starts = ends - padded
    rank = jnp.sum((jnp.cumsum(onehot, 0) - onehot) * onehot, -1)
    token_slot = (jnp.sum(starts[None, :] * onehot, -1) + rank).astype(jnp.int32)
    order = jnp.argsort(grp, stable=True).astype(jnp.int32)
    first = jnp.cumsum(counts) - counts
    wide = jnp.concatenate([jnp.zeros((n_slots,), jnp.int32), order, jnp.zeros((n_slots,), jnp.int32)])
    slot = jnp.arange(n_slots, dtype=jnp.int32)
    slot_token = jnp.zeros((n_slots,), jnp.int32)
    for g in range(N_EXPERT_GROUPS):
        run = lax.dynamic_slice(wide, (n_slots + first[g] - starts[g],), (n_slots,))
        inside = jnp.logical_and(slot >= starts[g], slot < starts[g] + counts[g])
        slot_token = jnp.where(inside, run, slot_token)
    tile_start = jnp.arange(n_slots // tm, dtype=jnp.int32) * tm
    tile_group = jnp.minimum(jnp.sum((tile_start[:, None] >= ends[None, :]).astype(jnp.int32), -1),
                             N_EXPERT_GROUPS - 1).astype(jnp.int32)
    tile_live = (tile_start < ends[-1]).astype(jnp.int32)
    return slot_token, token_slot, tile_group, tile_live


def _rope_tables(pos):
    half = QK_ROPE_DIM // 2
    inv_freq = ROPE_THETA ** (-jnp.arange(half, dtype=F32) / half)
    ang = pos.astype(F32)[:, None] * inv_freq[None, :]
    cos, sin = jnp.cos(ang), jnp.sin(ang)
    t = pos.shape[0]
    tail = HEAD_LANES - ROPE_OFF - QK_ROPE_DIM
    cos_t = jnp.concatenate([jnp.ones((t, ROPE_OFF), F32), cos, cos, jnp.zeros((t, tail), F32)], -1)
    sin_t = jnp.concatenate([jnp.zeros((t, ROPE_OFF), F32), sin, sin, jnp.zeros((t, tail), F32)], -1)
    return cos_t, sin_t


def _rot_partner(w):
    half = QK_ROPE_DIM // 2
    return jnp.concatenate([-w[..., half:], w[..., :half]], -1)


def _head_block(nope, rope):
    lead = nope.shape[:-1] if nope is not None else rope.shape[:-1]
    parts = [nope if nope is not None else jnp.zeros(lead + (QK_NOPE_DIM,), rope.dtype),
             rope if rope is not None else jnp.zeros(lead + (QK_ROPE_DIM,), nope.dtype)]
    parts.append(jnp.zeros(lead + (HEAD_LANES - QK_NOPE_DIM - QK_ROPE_DIM,), parts[0].dtype))
    out = jnp.concatenate(parts, -1)
    return out.reshape(out.shape[:-2] + (out.shape[-2] * HEAD_LANES,))


def _prep_weights(w_in, w_uq, w_uk, w_uv, w_pool, w_o, w_router, w_gate, w_up, w_down):
    depth = w_in.shape[0]
    o_pe = POOL_DIM + Q_LORA_RANK + KV_LORA_RANK
    w_pe = w_in[:, :, o_pe:]
    w_in_ext = jnp.concatenate([w_in[:, :, :o_pe], _head_block(None, w_pe[:, :, None, :]),
                                _head_block(None, _rot_partner(w_pe)[:, :, None, :])], -1).astype(BF16)
    uq = w_uq.reshape(depth, Q_LORA_RANK, N_HEADS, QK_NOPE_DIM + QK_ROPE_DIM)
    w_uq_pad = _head_block(uq[..., :QK_NOPE_DIM], uq[..., QK_NOPE_DIM:]).astype(BF16)
    w_uq_rot = _head_block(None, _rot_partner(uq[..., QK_NOPE_DIM:])).astype(BF16)
    w_uk_pad = _head_block(w_uk, None).astype(BF16)
    pad = jnp.zeros(w_uv.shape[:2] + (N_HEADS // 2, HEAD_LANES - V_HEAD_DIM), w_uv.dtype)
    uv = w_uv.reshape(depth, KV_LORA_RANK, N_HEADS // 2, 2, V_HEAD_DIM)
    w_uv_flat = jnp.concatenate([uv[:, :, :, 0], pad, pad, uv[:, :, :, 1]], -1).reshape(
        depth, KV_LORA_RANK, VA_LANES).astype(BF16)
    n_g = len(POOL_WINDOWS)
    eye = jnp.eye(n_g, dtype=F32)
    w_pool_bd = (w_pool[:, :, :, None, :] * eye[None, :, None, :, None]).reshape(depth, POOL_DIM, POOL_DIM).astype(BF16)
    r_hi, r_lo = _split_bf16(w_router.T)
    w_r = jnp.concatenate([r_hi, r_lo], 0)
    d = w_gate.shape[2]
    w_g = w_gate.reshape(depth * N_EXPERTS, d, D_EXPERT)
    w_u = w_up.reshape(depth * N_EXPERTS, d, D_EXPERT)
    w_dn = w_down.reshape(depth * N_EXPERTS, D_EXPERT, d)
    return w_in_ext, w_uq_pad, w_uq_rot, w_uk_pad, w_uv_flat, w_pool_bd, w_o.astype(BF16), w_r, w_g, w_u, w_dn


def _mod_blocks(mod_rows, d):
    depth, bsz, _ = mod_rows.shape
    m = mod_rows.reshape(depth, bsz, 6, d)
    return jnp.concatenate([m, jnp.zeros((depth, bsz, 2, d), F32)], 2)


def kernel(x_prompt, x_sample, cache_kv, cache_pe, state_pool, c_prompt, c_sample, ln_in_g, ln_in_b, w_ada, b_ada,
           w_in, q_norm_g, kv_norm_g, w_uq, w_uk, w_uv, w_pool, pool_scale, w_o, ln1_g, ln1_b, w_router,
           router_bias, w_gate, w_up, w_down, ln2_g, ln2_b):
    depth = w_in.shape[0]
    n_p, t_p, d = x_prompt.shape
    n_s, t_s, _ = x_sample.shape
    past = cache_kv.shape[2]
    alpha = float((2 * depth) ** 0.25)

    (w_in_ext, w_uq_pad, w_uq_rot, w_uk_pad, w_uv_flat, w_pool_bd, w_o_b, w_r, w_g, w_u, w_dn) = _prep_weights(
        w_in, w_uq, w_uk, w_uv, w_pool, w_o, w_router, w_gate, w_up, w_down)
    row = lambda a: a.reshape(1, -1)
    rb_t = router_bias.reshape(N_EXPERTS, 1)
    head_lane = jnp.arange(VA_LANES, dtype=jnp.int32) % HEAD_LANES
    odd_head = (jnp.arange(VA_LANES, dtype=jnp.int32) // HEAD_LANES) % 2 == 1
    ones_row = (head_lane == jnp.where(odd_head, ONES_LANE[1], ONES_LANE[0])).astype(F32).reshape(1, VA_LANES)

    n_c = n_p + n_s
    rows = -(-n_c // 8) * 8
    c_all = jnp.concatenate([c_prompt, c_sample, jnp.zeros((rows - n_c, d), F32)], 0)
    mod = _ada_call(c_all, w_ada, b_ada)
    mod_p = _mod_blocks(mod[:, :n_p], d)
    mod_s = _mod_blocks(mod[:, n_p:n_c], d)

    cos_p, sin_p = _rope_tables(jnp.arange(t_p, dtype=jnp.int32))
    cos_s, sin_s = _rope_tables(past + jnp.arange(t_s, dtype=jnp.int32))
    cache_pe_blk = _head_block(None, cache_pe[:, :, :, None, :])
    left_p = jnp.zeros((depth, n_p, HALO, POOL_DIM), F32)
    left_s = jnp.concatenate([jnp.zeros((depth, n_s, HALO - POOL_CTX, POOL_DIM), F32), state_pool], 2)

    tt_p = min(512, t_p)
    tt_s = t_s

    def run(x, mod_g, cos, sin, left, tt, pos0, prompt):
        kvs, pes, pools = [], [], []
        for l in range(depth):
            w_pre = (w_in_ext[l], row(q_norm_g[l]), row(kv_norm_g[l]), w_uq_pad[l], w_uq_rot[l], w_uk_pad[l],
                     w_uv_flat[l], ones_row)
            outs = _pre_call(x, mod_g[l], (row(ln_in_g), row(ln_in_b)), w_pre, cos, sin, first=(l == 0), tt=tt)
            if l == 0:
                x = outs[0]
                outs = outs[1:]
            u, ckv, kpe, q, k, v = outs
            if prompt:
                att = _prompt_attn_call(q, k, v, tq=min(ATTN_TILE, x.shape[1]))
            else:
                att = _sample_attn_call(q, k, v, cache_kv, cache_pe_blk, w_uk_pad[l], w_uv_flat[l], ones_row, l)
            w_post = (w_pool_bd[l], row(pool_scale[l]), w_o_b[l], row(ln1_g[l]), row(ln1_b[l]), w_r, rb_t)
            x1, hg = _post_call(x, u, left[l], att, mod_g[l], w_post, tt=tt, pos0=pos0, alpha=alpha)
            g2 = mod_g[l][:, 5:6, :]
            bsz, t, _ = x1.shape
            if prompt:
                hg_rows = hg.reshape(bsz * t, d + GATE_LANES)
                slot_token, token_slot, tile_group, tile_live = _dispatch_plan(hg_rows[:, d:d + N_EXPERTS], MOE_TILE)
                ys = _moe_grouped_call(hg_rows, slot_token, tile_group, tile_live, w_g, w_u, w_dn, l, tm=MOE_TILE)
                x = _ln_gather_call(x1, ys, token_slot, g2, row(ln2_g[l]), row(ln2_b[l]), tt=tt, alpha=alpha)
            else:
                flat = lambda a: a.reshape(1, bsz * t, a.shape[-1])
                g2_tok = jnp.broadcast_to(g2, (bsz, t, d))
                x = _moe_call(flat(x1), flat(hg), flat(g2_tok), w_g, w_u, w_dn, row(ln2_g[l]),
                              row(ln2_b[l]), l, tt=bsz * t, alpha=alpha).reshape(bsz, t, d)
            kvs.append(ckv)
            pes.append(kpe[:, :, ROPE_OFF:ROPE_OFF + QK_ROPE_DIM])
            pools.append(u[:, -POOL_CTX:, :])
        return x, jnp.stack(kvs), jnp.stack(pes), jnp.stack(pools)

    y_p, kv_p, pe_p, pool_p = run(x_prompt, mod_p, cos_p, sin_p, left_p, tt_p, 0, True)
    y_s, kv_s, pe_s, pool_s = run(x_sample, mod_s, cos_s, sin_s, left_s, tt_s, past, False)
    return (y_p, y_s, kv_p, pe_p, pool_p, kv_s, pe_s, pool_s)
```
